```python
import jax, jax.numpy as jnp
from jax import lax
import numpy as np

D_MODEL = 1024
BATCH = 32
SEQ = 256
DEPTH = 1
DEC_BATCH = 4
DEC_SEQ = 4096
PAST_LEN = 512

GRID_W = 64
CHUNK = 128
D_A = 1024
H_A = 4
DH_A = D_A // H_A
D_B = 1024
G_B = 4
DG_B = D_B // G_B
N_IF = 2 * 2 * H_A
N_IN = 5 * D_A + N_IF + 3 * D_B + 2 * D_MODEL
EPS = 1e-6

kernel_name = "hybrid_mlstm_gmlp_diffusion_step"


def rmsnorm(x, g):
    xf = x.astype(jnp.float32)
    y = xf * lax.rsqrt(jnp.mean(xf * xf, axis=-1, keepdims=True) + EPS)
    return (y * g.astype(jnp.float32)).astype(x.dtype)


def split_cols(p):
    sizes = (D_A, D_A, D_A, D_A, D_A, N_IF, D_B, D_B, D_B, D_MODEL, D_MODEL)
    outs, start = [], 0
    for s in sizes:
        outs.append(p[..., start:start + s])
        start += s
    return outs


def mlstm_chunked(q, k, v, li, lf, C0, n0, m0, n_chunks):
    B, H, T, d = q.shape
    L = T // n_chunks
    q = q.reshape(B, H, n_chunks, L, d)
    k = k.reshape(B, H, n_chunks, L, d)
    v = v.reshape(B, H, n_chunks, L, d)
    li = li.reshape(B, H, n_chunks, L)
    lf = lf.reshape(B, H, n_chunks, L)
    b = jnp.cumsum(lf, axis=-1)
    bL = b[..., -1]
    w = bL[..., None] - b + li
    m_loc = jnp.max(w, axis=-1)
    e = jnp.exp(w - m_loc[..., None])
    C_loc = jnp.einsum('bhcs,bhcsv,bhcsk->bhcvk', e, v, k)
    n_loc = jnp.einsum('bhcs,bhcsk->bhck', e, k)

    def step(carry, inp):
        C, n, m = carry
        bL_c, ml_c, Cl_c, nl_c = inp
        m_new = jnp.maximum(bL_c + m, ml_c)
        sp = jnp.exp(bL_c + m - m_new)
        sl = jnp.exp(ml_c - m_new)
        C_new = sp[..., None, None] * C + sl[..., None, None] * Cl_c
        n_new = sp[..., None] * n + sl[..., None] * nl_c
        return (C_new, n_new, m_new), (C, n, m)

    mv = lambda t: jnp.moveaxis(t, 2, 0)
    (Cf, nf, mf), (Cs, ns, ms) = lax.scan(step, (C0, n0, m0), (mv(bL), mv(m_loc), mv(C_loc), mv(n_loc)))
    Cs = jnp.moveaxis(Cs, 0, 2)
    ns = jnp.moveaxis(ns, 0, 2)
    ms = jnp.moveaxis(ms, 0, 2)

    a = b + ms[..., None]
    Dm = b[..., :, None] - b[..., None, :] + li[..., None, :]
    causal = jnp.tril(jnp.ones((L, L), dtype=bool))
    Dm = jnp.where(causal, Dm, -jnp.inf)
    m_t = jnp.maximum(a, jnp.max(Dm, axis=-1))
    S = jnp.einsum('bhctd,bhcsd->bhcts', q, k) * jnp.exp(Dm - m_t[..., None])
    sc = jnp.exp(a - m_t)
    num = jnp.einsum('bhcts,bhcsv->bhctv', S, v) + sc[..., None] * jnp.einsum('bhcvk,bhctk->bhctv', Cs, q)
    den = jnp.sum(S, axis=-1) + sc * jnp.einsum('bhck,bhctk->bhct', ns, q)
    h = num / jnp.maximum(jnp.abs(den), jnp.exp(-m_t))[..., None]
    return h.reshape(B, H, T, d), (Cf, nf, mf)


def mlstm_bidir(q, k, v, if_pre, C0, n0, m0, n_chunks):
    li = jnp.transpose(if_pre[:, :, :, 0, :], (0, 2, 3, 1))
    lf = jax.nn.log_sigmoid(jnp.transpose(if_pre[:, :, :, 1, :], (0, 2, 3, 1)))
    h_f, (Cf, nf, mf) = mlstm_chunked(q, k, v, li[:, 0], lf[:, 0], C0[:, 0], n0[:, 0], m0[:, 0], n_chunks)
    rev = lambda t: jnp.flip(t, axis=2)
    h_b, (Cb, nb, mb) = mlstm_chunked(rev(q), rev(k), rev(v), jnp.flip(li[:, 1], -1), jnp.flip(lf[:, 1], -1),
                                      C0[:, 1], n0[:, 1], m0[:, 1], n_chunks)
    h = h_f + rev(h_b)
    return h, jnp.stack([Cf, Cb], axis=1), jnp.stack([nf, nb], axis=1), jnp.stack([mf, mb], axis=1)


def spatial_gate(v, g_sgu, w_s, b_s, n_chunks):
    B, T, _ = v.shape
    vn = rmsnorm(v, g_sgu).reshape(B, n_chunks, T // n_chunks, G_B, DG_B)
    s = jnp.einsum('gts,bcsgd->bctgd', w_s, vn) + jnp.transpose(b_s)[:, :, None]
    return s.reshape(B, T, D_B)


def mixer_layer(x, mod, C0, n0, m0, n_chunks, norm_g, w_in, b_if, g_sgu, w_s, b_s, w_proj_a, w_proj_b, w_out):
    B, T, _ = x.shape
    shift, scale, gate = jnp.split(mod, 3, axis=-1)
    h = rmsnorm(x, norm_g) * (1 + scale[:, None]) + shift[:, None]
    q, k, v, o, z_a, g_if, u, v_b, z_b, g_a, g_b = split_cols(h @ w_in)
    heads = lambda t: jnp.transpose(t.reshape(B, T, H_A, DH_A), (0, 2, 1, 3)).astype(jnp.float32)
    if_pre = (g_if.reshape(B, T, 2, 2, H_A) + b_if).astype(jnp.float32)
    h_a, Cn, nn_, mn = mlstm_bidir(heads(q) * (DH_A ** -0.5), heads(k), heads(v), if_pre, C0, n0, m0, n_chunks)
    h_a = jnp.transpose(h_a, (0, 2, 1, 3)).reshape(B, T, D_A).astype(x.dtype)
    y_a = jax.nn.sigmoid(o) * h_a * jax.nn.silu(z_a)
    y_b = u * spatial_gate(v_b, g_sgu, w_s, b_s, n_chunks) * jax.nn.silu(z_b)
    merged = jax.nn.sigmoid(g_a) * (y_a @ w_proj_a) + jax.nn.sigmoid(g_b) * (y_b @ w_proj_b)
    return x + gate[:, None] * (merged @ w_out), Cn, nn_, mn


def setup_inputs(seed: int = 0) -> dict:
    key = jax.random.key(seed)
    ks = jax.random.split(key, 20)
    nrm = lambda k, shape, s: jax.random.normal(k, shape, jnp.float32) * s
    f_bias = jnp.linspace(3.0, 6.0, H_A, dtype=jnp.float32)
    b_if = jnp.stack([nrm(ks[9], (DEPTH, 2, H_A), 0.1),
                      f_bias + nrm(ks[10], (DEPTH, 2, H_A), 0.1)], axis=2)
    return {
        "x_prompt": nrm(ks[0], (BATCH, SEQ, D_MODEL), 1.0),
        "x_sample": nrm(ks[1], (DEC_BATCH, DEC_SEQ, D_MODEL), 1.0),
        "c": nrm(ks[2], (DEC_BATCH, D_MODEL), 1.0),
        "state_C": nrm(ks[3], (DEC_BATCH, DEPTH, 2, H_A, DH_A, DH_A), 0.5),
        "state_n": nrm(ks[4], (DEC_BATCH, DEPTH, 2, H_A, DH_A), 0.5),
        "state_m": nrm(ks[5], (DEC_BATCH, DEPTH, 2, H_A), 1.0),
        "c_ctx": nrm(ks[6], (D_MODEL,), 1.0),
        "norm_g": 1.0 + nrm(ks[7], (DEPTH, D_MODEL), 0.02),
        "w_ada": nrm(ks[8], (DEPTH, D_MODEL, 3 * D_MODEL), 0.5 * D_MODEL ** -0.5),
        "b_ada": nrm(ks[11], (DEPTH, 3 * D_MODEL), 0.02),
        "w_in": nrm(ks[12], (DEPTH, D_MODEL, N_IN), D_MODEL ** -0.5),
        "b_if": b_if,
        "g_sgu": 1.0 + nrm(ks[13], (DEPTH, D_B), 0.02),
        "w_s": nrm(ks[14], (DEPTH, G_B, CHUNK, CHUNK), CHUNK ** -0.5),
        "b_s": 1.0 + nrm(ks[15], (DEPTH, G_B, CHUNK), 0.1),
        "w_proj_a": nrm(ks[16], (DEPTH, D_A, D_MODEL), D_A ** -0.5),
        "w_proj_b": nrm(ks[17], (DEPTH, D_B, D_MODEL), D_B ** -0.5),
        "w_out": nrm(ks[18], (DEPTH, D_MODEL, D_MODEL), D_MODEL ** -0.5),
        "norm_f": 1.0 + nrm(ks[19], (D_MODEL,), 0.02),
    }


def reference(x_prompt, x_sample, c, state_C, state_n, state_m, c_ctx, norm_g, w_ada, b_ada, w_in, b_if,
              g_sgu, w_s, b_s, w_proj_a, w_proj_b, w_out, norm_f):
    Bp, Tp, _ = x_prompt.shape
    Bs, Ts, _ = x_sample.shape
    rows = Ts // GRID_W
    nc_ctx = Tp // CHUNK
    nc_lat = (rows * GRID_W) // CHUNK
    f32 = jnp.float32
    C_ctx = jnp.zeros((Bp, 2, H_A, DH_A, DH_A), f32)
    n_ctx = jnp.zeros((Bp, 2, H_A, DH_A), f32)
    m_ctx = jnp.zeros((Bp, 2, H_A), f32)
    xp, xs = x_prompt, x_sample
    new_C, new_n, new_m = [], [], []
    for l in range(DEPTH):
        lw = (norm_g[l], w_in[l], b_if[l], g_sgu[l], w_s[l], b_s[l], w_proj_a[l], w_proj_b[l], w_out[l])
        mod_ctx = jax.nn.silu(c_ctx)[None] @ w_ada[l] + b_ada[l]
        mod_lat = jax.nn.silu(c) @ w_ada[l] + b_ada[l]
        xp, Cn, nn_, mn = mixer_layer(xp, mod_ctx, C_ctx, n_ctx, m_ctx, nc_ctx, *lw)
        new_C.append(Cn.astype(x_prompt.dtype))
        new_n.append(nn_.astype(x_prompt.dtype))
        new_m.append(mn.astype(x_prompt.dtype))
        xs, _, _, _ = mixer_layer(xs, mod_lat, state_C[:, l].astype(f32), state_n[:, l].astype(f32),
                                  state_m[:, l].astype(f32), nc_lat, *lw)
    y_prompt = rmsnorm(xp, norm_f)
    y_sample = rmsnorm(xs, norm_f)
    new_state_C = jnp.stack(new_C, axis=1)
    new_state_n = jnp.stack(new_n, axis=1)
    new_state_m = jnp.stack(new_m, axis=1)
    return (y_prompt, y_sample, new_state_C, new_state_n, new_state_m)
```

```python
import functools

import jax
import jax.numpy as jnp
from jax import lax
from jax.experimental import pallas as pl
from jax.experimental.pallas import tpu as pltpu

D_MODEL = 1024
H_A = 4
DH_A = 256
G_B = 4
DG_B = 256
CHUNK = 128
N_GATES = 16
EPS = 1e-6
MASKED = -1e30
F32 = jnp.float32
BF16 = jnp.bfloat16
HIGHEST = lax.Precision.HIGHEST
VMEM_LIMIT_BYTES = 60000 * 1024

N_PROJ = 10
P_Q, P_K, P_V, P_O, P_ZA, P_U, P_VB, P_ZB, P_GA, P_GB = range(N_PROJ)


def _sigmoid(x):
    return 1.0 / (1.0 + jnp.exp(-x))


def _log_sigmoid(x):
    return jnp.minimum(x, 0.0) - jnp.log1p(jnp.exp(-jnp.abs(x)))


def _nt_dot(a, b):
    return lax.dot_general(a, b, (((1,), (1,)), ((), ())), preferred_element_type=F32)


def _mod_kernel(c_ref, w_ref, b_ref, o_ref):
    cv = c_ref[...]
    act = cv * _sigmoid(cv)
    o_ref[...] = jnp.dot(act, w_ref[...], precision=HIGHEST, preferred_element_type=F32) + b_ref[...]


def _modulation(cvecs, w_ada, b_ada):
    n_col = 512
    return pl.pallas_call(
        _mod_kernel,
        grid=(3 * D_MODEL // n_col,),
        in_specs=[
            pl.BlockSpec((8, D_MODEL), lambda j: (0, 0)),
            pl.BlockSpec((D_MODEL, n_col), lambda j: (0, j)),
            pl.BlockSpec((1, n_col), lambda j: (0, j)),
        ],
        out_specs=pl.BlockSpec((8, n_col), lambda j: (0, j)),
        out_shape=jax.ShapeDtypeStruct((8, 3 * D_MODEL), F32),
        name="modulation",
    )(cvecs, w_ada, b_ada)


def _inproj_kernel(x_ref, mod_ref, ng_ref, w_ref, wif_ref, wift_ref, bif_ref, bift_ref, gsgu_ref, ws_ref, bs_ref,
                   wpb_ref, q_ref, k_ref, v_ref, oz_ref, sga_ref, pb_ref, gc_ref, gr_ref, yb_scr, *, tm):
    x = x_ref[...]
    xn = x * lax.rsqrt(jnp.mean(x * x, axis=-1, keepdims=True) + EPS) * ng_ref[...]
    shift = mod_ref[0, 0:1, :]
    scale = mod_ref[0, 1:2, :]
    hb = (xn * (1.0 + scale) + shift).astype(BF16)

    def proj(j):
        return jnp.dot(hb, w_ref[j], preferred_element_type=F32)

    q_ref[...] = (proj(P_Q) * (DH_A ** -0.5)).astype(BF16)
    k_ref[...] = proj(P_K).astype(BF16)
    v_ref[...] = proj(P_V).astype(BF16)
    za = proj(P_ZA)
    oz_ref[...] = (_sigmoid(proj(P_O)) * (za * _sigmoid(za))).astype(BF16)
    gc_ref[...] = jnp.dot(hb, wif_ref[...], preferred_element_type=F32) + bif_ref[...]
    gr_ref[...] = _nt_dot(wift_ref[...], hb) + bift_ref[...]

    vb = proj(P_VB)
    vn = (vb * lax.rsqrt(jnp.mean(vb * vb, axis=-1, keepdims=True) + EPS) * gsgu_ref[...]).astype(BF16)
    u = proj(P_U)
    zb = proj(P_ZB)
    uz = u * (zb * _sigmoid(zb))
    for c in range(tm // CHUNK):
        rows = slice(c * CHUNK, (c + 1) * CHUNK)
        for g in range(G_B):
            cols = slice(g * DG_B, (g + 1) * DG_B)
            s = jnp.dot(ws_ref[g], vn[rows, cols], preferred_element_type=F32) + bs_ref[:, cols]
            yb_scr[rows, cols] = (uz[rows, cols] * s).astype(BF16)
    pb = _sigmoid(proj(P_GB)) * jnp.dot(yb_scr[...], wpb_ref[...], preferred_element_type=F32)
    pb_ref[...] = pb.astype(BF16)
    sga_ref[...] = _sigmoid(proj(P_GA)).astype(BF16)


def _inproj(x2d, mod3, seq_tiles, ng, w10, wif, wift, bif, bift, gsgu, ws, bs_full, wpb, *, tm):
    n_tok = x2d.shape[0]
    const2 = lambda i: (0, 0)
    const3 = lambda i: (0, 0, 0)
    resident = dict(pipeline_mode=pl.Buffered(1))
    tok_spec = pl.BlockSpec((tm, D_MODEL), lambda i: (i, 0))
    if mod3.shape[0] == 1:
        mod_map = const3
    else:
        mod_map = lambda i: (i // seq_tiles, 0, 0)
    tok_out = jax.ShapeDtypeStruct((n_tok, D_MODEL), BF16)
    return pl.pallas_call(
        functools.partial(_inproj_kernel, tm=tm),
        grid=(n_tok // tm,),
        in_specs=[
            tok_spec,
            pl.BlockSpec((1, 3, D_MODEL), mod_map),
            pl.BlockSpec((1, D_MODEL), const2),
            pl.BlockSpec((N_PROJ, D_MODEL, D_MODEL), const3, **resident),
            pl.BlockSpec((D_MODEL, N_GATES), const2),
            pl.BlockSpec((N_GATES, D_MODEL), const2),
            pl.BlockSpec((1, N_GATES), const2),
            pl.BlockSpec((N_GATES, 1), const2),
            pl.BlockSpec((1, D_MODEL), const2),
            pl.BlockSpec((G_B, CHUNK, CHUNK), const3),
            pl.BlockSpec((CHUNK, D_MODEL), const2),
            pl.BlockSpec((D_MODEL, D_MODEL), const2, **resident),
        ],
        out_specs=[tok_spec] * 6 + [
            pl.BlockSpec((tm, N_GATES), lambda i: (i, 0)),
            pl.BlockSpec((N_GATES, tm), lambda i: (0, i)),
        ],
        out_shape=[tok_out] * 6 + [
            jax.ShapeDtypeStruct((n_tok, N_GATES), F32),
            jax.ShapeDtypeStruct((N_GATES, n_tok), F32),
        ],
        scratch_shapes=[pltpu.VMEM((tm, D_MODEL), BF16)],
        compiler_params=pltpu.CompilerParams(
            dimension_semantics=("arbitrary",), vmem_limit_bytes=VMEM_LIMIT_BYTES),
        name="inproj",
    )(x2d, mod3, ng, w10, wif, wift, bif, bift, gsgu, ws, bs_full, wpb)


def _mlstm_sweep(d, blk, tblk, q_ref, k_ref, v_ref, gc_ref, gr_ref, c_scr, n_scr, m_scr, emit):
    n_chunks = tblk // CHUNK
    ti = lax.broadcasted_iota(jnp.int32, (CHUNK, CHUNK), 0)
    si = lax.broadcasted_iota(jnp.int32, (CHUNK, CHUNK), 1)
    if d == 0:
        mask = si <= ti
        mask_t = ti <= si
    else:
        mask = si >= ti
        mask_t = ti >= si
    tri = mask.astype(F32)
    tri_t = mask_t.astype(F32)
    last = CHUNK - 1 if d == 0 else 0

    def body(i, carry):
        c = i if d == 0 else n_chunks - 1 - i
        r0 = pl.multiple_of(c * CHUNK, CHUNK)
        gcol = gc_ref[pl.ds(r0, CHUNK), :]
        grow = gr_ref[:, pl.ds(r0, CHUNK)]
        cum_col = jnp.dot(tri, _log_sigmoid(gcol), precision=HIGHEST, preferred_element_type=F32)
        cum_row = jnp.dot(_log_sigmoid(grow), tri_t, precision=HIGHEST, preferred_element_type=F32)
        for h in range(H_A):
            ji = d * 8 + h
            jf = d * 8 + 4 + h
            hcols = slice(h * DH_A, (h + 1) * DH_A)
            cc = cum_col[:, jf:jf + 1]
            rr = grow[ji:ji + 1, :] - cum_row[jf:jf + 1, :]
            b_end = cum_row[jf:jf + 1, last:last + 1]
            m_prev = m_scr[d, h:h + 1, 0:1]
            qh = q_ref[pl.ds(r0, CHUNK), hcols]
            kh = k_ref[pl.ds(r0, CHUNK), hcols]
            vh = v_ref[pl.ds(r0, CHUNK), hcols]

            dm = jnp.where(mask, cc + rr, MASKED)
            a = cc + m_prev
            m_t = jnp.maximum(a, jnp.max(dm, axis=1, keepdims=True))
            s = _nt_dot(qh, kh) * jnp.exp(dm - m_t)
            sc = jnp.exp(a - m_t)
            c_prev = c_scr[d, h]
            n_prev = n_scr[d, h:h + 1, :]
            inter = _nt_dot(qh, c_prev.astype(BF16))
            num = jnp.dot(s.astype(BF16), vh, preferred_element_type=F32) + sc * inter
            nq = jnp.sum(qh.astype(F32) * n_prev, axis=1, keepdims=True)
            den = jnp.sum(s, axis=1, keepdims=True) + sc * nq
            emit(r0, h, num * (1.0 / jnp.maximum(jnp.abs(den), jnp.exp(-m_t))))

            w_row = b_end + rr
            m_loc = jnp.max(w_row, axis=1, keepdims=True)
            e_row = jnp.exp(w_row - m_loc)
            vte = (vh.astype(F32).T * e_row).astype(BF16)
            c_loc = jnp.dot(vte, kh, preferred_element_type=F32)
            n_loc = jnp.dot(jnp.broadcast_to(e_row, (8, CHUNK)).astype(BF16), kh,
                            preferred_element_type=F32)[0:1, :]
            m_new = jnp.maximum(b_end + m_prev, m_loc)
            sp = jnp.exp(b_end + m_prev - m_new)
            sl = jnp.exp(m_loc - m_new)
            c_scr[d, h] = sp * c_prev + sl * c_loc
            n_scr[d, h:h + 1, :] = sp * n_prev + sl * n_loc
            m_scr[d, h:h + 1, :] = jnp.broadcast_to(m_new, (1, CHUNK))
        return carry

    lax.fori_loop(0, n_chunks, body, 0)


def _mixer_kernel(*refs, nblk, tblk, has_init, emit_state):
    refs = list(refs)
    q_ref, k_ref, v_ref, gc_ref, gr_ref, oz_ref, sga_ref, pb_ref, x_ref, mod_ref = refs[:10]
    refs = refs[10:]
    if has_init:
        c0_ref, n0_ref, m0_ref = refs[:3]
        refs = refs[3:]
    wpa_ref, wout_ref, nf_ref, y_ref = refs[:4]
    refs = refs[4:]
    if emit_state:
        cf_ref, nfin_ref, mfin_ref = refs[:3]
        refs = refs[3:]
    c_scr, n_scr, m_scr, hb_scr, ya_scr = refs

    step = pl.program_id(1)
    sweep = functools.partial(_mlstm_sweep, tblk=tblk, q_ref=q_ref, k_ref=k_ref, v_ref=v_ref, gc_ref=gc_ref,
                              gr_ref=gr_ref, c_scr=c_scr, n_scr=n_scr, m_scr=m_scr)

    @pl.when(step == 0)
    def _():
        if has_init:
            c_scr[...] = c0_ref[0]
            n_scr[...] = n0_ref[0]
            m_scr[...] = m0_ref[0]
        else:
            c_scr[...] = jnp.zeros_like(c_scr)
            n_scr[...] = jnp.zeros_like(n_scr)
            m_scr[...] = jnp.zeros_like(m_scr)

    @pl.when(step < nblk)
    def _():
        blk = nblk - 1 - step
        base = pl.multiple_of(blk * tblk, CHUNK)

        def emit(r0, h, val):
            hb_scr[pl.ds(base + r0, CHUNK), h * DH_A:(h + 1) * DH_A] = val.astype(BF16)

        sweep(1, blk, emit=emit)

    @pl.when(step >= nblk)
    def _():
        blk = step - nblk
        base = pl.multiple_of(blk * tblk, CHUNK)

        def emit(r0, h, val):
            hcols = slice(h * DH_A, (h + 1) * DH_A)
            h_a = val + hb_scr[pl.ds(base + r0, CHUNK), hcols].astype(F32)
            ya_scr[pl.ds(r0, CHUNK), hcols] = (oz_ref[pl.ds(r0, CHUNK), hcols].astype(F32) * h_a).astype(BF16)

        sweep(0, blk, emit=emit)

        pa = jnp.dot(ya_scr[...], wpa_ref[...], preferred_element_type=F32)
        merged = sga_ref[...].astype(F32) * pa + pb_ref[...].astype(F32)
        upd = jnp.dot(merged.astype(BF16), wout_ref[...], preferred_element_type=F32)
        out = x_ref[...] + mod_ref[0, 2:3, :] * upd
        y_ref[...] = out * lax.rsqrt(jnp.mean(out * out, axis=-1, keepdims=True) + EPS) * nf_ref[...]

    if emit_state:
        @pl.when(step == 2 * nblk - 1)
        def _():
            cf_ref[0] = c_scr[...]
            nfin_ref[0] = n_scr[...]
            mfin_ref[0] = m_scr[...]


def _mixer(q, k, v, oz, sga, pb, gc, gr, x2d, mod3, init_state, wpa, wout, norm_f, *, n_seq, seq_len, tblk,
           emit_state):
    nblk = seq_len // tblk
    has_init = init_state is not None

    def sweep_blk(b, s):
        return b * nblk + jnp.where(s < nblk, nblk - 1 - s, s - nblk)

    def tail_blk(b, s):
        return b * nblk + jnp.maximum(s - nblk, 0)

    sweep_spec = pl.BlockSpec((tblk, D_MODEL), lambda b, s: (sweep_blk(b, s), 0))
    tail_spec = pl.BlockSpec((tblk, D_MODEL), lambda b, s: (tail_blk(b, s), 0))
    const2 = lambda b, s: (0, 0)
    if mod3.shape[0] == 1:
        mod_map = lambda b, s: (0, 0, 0)
    else:
        mod_map = lambda b, s: (b, 0, 0)
    c_spec = pl.BlockSpec((1, 2, H_A, DH_A, DH_A), lambda b, s: (b, 0, 0, 0, 0))
    n_spec = pl.BlockSpec((1, 2, H_A, DH_A), lambda b, s: (b, 0, 0, 0))
    m_spec = pl.BlockSpec((1, 2, H_A, CHUNK), lambda b, s: (b, 0, 0, 0))

    in_specs = [
        sweep_spec, sweep_spec, sweep_spec,
        pl.BlockSpec((tblk, N_GATES), lambda b, s: (sweep_blk(b, s), 0)),
        pl.BlockSpec((N_GATES, tblk), lambda b, s: (0, sweep_blk(b, s))),
        tail_spec, tail_spec, tail_spec, tail_spec,
        pl.BlockSpec((1, 3, D_MODEL), mod_map),
    ]
    args = [q, k, v, gc, gr, oz, sga, pb, x2d, mod3]
    if has_init:
        in_specs += [c_spec, n_spec, m_spec]
        args += list(init_state)
    in_specs += [
        pl.BlockSpec((D_MODEL, D_MODEL), const2),
        pl.BlockSpec((D_MODEL, D_MODEL), const2),
        pl.BlockSpec((1, D_MODEL), const2),
    ]
    args += [wpa, wout, norm_f]

    out_specs = [tail_spec]
    out_shape = [jax.ShapeDtypeStruct((n_seq * seq_len, D_MODEL), F32)]
    if emit_state:
        out_specs += [c_spec, n_spec, m_spec]
        out_shape += [
            jax.ShapeDtypeStruct((n_seq, 2, H_A, DH_A, DH_A), F32),
            jax.ShapeDtypeStruct((n_seq, 2, H_A, DH_A), F32),
            jax.ShapeDtypeStruct((n_seq, 2, H_A, CHUNK), F32),
        ]

    return pl.pallas_call(
        functools.partial(_mixer_kernel, nblk=nblk, tblk=tblk, has_init=has_init, emit_state=emit_state),
        grid=(n_seq, 2 * nblk),
        in_specs=in_specs,
        out_specs=out_specs,
        out_shape=out_shape,
        scratch_shapes=[
            pltpu.VMEM((2, H_A, DH_A, DH_A), F32),
            pltpu.VMEM((2, H_A, DH_A), F32),
            pltpu.VMEM((2, H_A, CHUNK), F32),
            pltpu.VMEM((seq_len, D_MODEL), BF16),
            pltpu.VMEM((tblk, D_MODEL), BF16),
        ],
        compiler_params=pltpu.CompilerParams(
            dimension_semantics=("arbitrary", "arbitrary"), vmem_limit_bytes=VMEM_LIMIT_BYTES),
        name="mixer",
    )(*args)


def _split_w_in(w):
    cols = [0, 1, 2, 3, 4]
    gate0 = 5 * D_MODEL
    rest0 = gate0 + N_GATES
    parts = [w[:, j * D_MODEL:(j + 1) * D_MODEL] for j in cols]
    parts += [w[:, rest0 + j * D_MODEL:rest0 + (j + 1) * D_MODEL] for j in range(5)]
    w10 = jnp.stack(parts, axis=0).astype(BF16)
    wif = w[:, gate0:rest0].astype(BF16)
    return w10, wif, wif.T


def kernel(x_prompt, x_sample, c, state_C, state_n, state_m, c_ctx, norm_g, w_ada, b_ada, w_in, b_if, g_sgu, w_s, b_s,
           w_proj_a, w_proj_b, w_out, norm_f):
    n_ctx, t_ctx, _ = x_prompt.shape
    n_lat, t_lat, _ = x_sample.shape
    depth = norm_g.shape[0]
    assert depth == 1, "single-layer step"
    l = 0

    cvecs = jnp.concatenate([c, c_ctx[None], jnp.zeros((8 - n_lat - 1, D_MODEL), F32)], axis=0)
    mod = _modulation(cvecs, w_ada[l], b_ada[l][None]).reshape(8, 3, D_MODEL)
    mod_lat = mod[:n_lat]
    mod_ctx = mod[n_lat:n_lat + 1]

    w10, wif, wift = _split_w_in(w_in[l])
    bif = b_if[l].reshape(1, N_GATES)
    bift = b_if[l].reshape(N_GATES, 1)
    ng = norm_g[l][None]
    gsgu = g_sgu[l][None]
    ws = w_s[l].astype(BF16)
    bs_full = jnp.repeat(jnp.transpose(b_s[l]), DG_B, axis=1)
    wpa = w_proj_a[l].astype(BF16)
    wpb = w_proj_b[l].astype(BF16)
    wout = w_out[l].astype(BF16)
    nf = norm_f[None]

    tm = 512
    xp2 = x_prompt.reshape(n_ctx * t_ctx, D_MODEL)
    xs2 = x_sample.reshape(n_lat * t_lat, D_MODEL)
    shared = (ng, w10, wif, wift, bif, bift, gsgu, ws, bs_full, wpb)
    pc = _inproj(xp2, mod_ctx, 1, *shared, tm=tm)
    ps = _inproj(xs2, mod_lat, t_lat // tm, *shared, tm=tm)

    y_p, c_new, n_new, m_new = _mixer(*pc, xp2, mod_ctx, None, wpa, wout, nf, n_seq=n_ctx, seq_len=t_ctx,
                                      tblk=t_ctx, emit_state=True)
    init = (state_C[:, l], state_n[:, l],
            jnp.broadcast_to(state_m[:, l][..., None], (n_lat, 2, H_A, CHUNK)))
    (y_s,) = _mixer(*ps, xs2, mod_lat, init, wpa, wout, nf, n_seq=n_lat, seq_len=t_lat, tblk=512,
                    emit_state=False)

    return (y_p.reshape(n_ctx, t_ctx, D_MODEL), y_s.reshape(n_lat, t_lat, D_MODEL),
            c_new[:, None], n_new[:, None], m_new[:, None, :, :, 0])
```

```python
import functools

import jax
import jax.numpy as jnp
from jax import lax
from jax.experimental import pallas as pl
from jax.experimental.pallas import tpu as pltpu

D_MODEL = 1024
H_A = 4
DH_A = 256
G_B = 4
DG_B = 256
CHUNK = 128
N_GATES = 16
EPS = 1e-6
MASKED = -1e30
F32 = jnp.float32
BF16 = jnp.bfloat16
HIGHEST = lax.Precision.HIGHEST
VMEM_LIMIT_BYTES = 60000 * 1024
BF16_ROWS = 16
CN_ROWS = DH_A + BF16_ROWS
GS_PER_DIR = 40
GS_ROWS = 2 * GS_PER_DIR
GS_CUM, GS_RR, GS_CMAX, GS_BEND, GS_MLOC = range(5)

N_PROJ = 10
P_Q, P_K, P_V, P_O, P_ZA, P_U, P_VB, P_ZB, P_GA, P_GB = range(N_PROJ)


def _sigmoid(x):
    return 1.0 / (1.0 + jnp.exp(-x))


def _log_sigmoid(x):
    return jnp.minimum(x, 0.0) - jnp.log1p(jnp.exp(-jnp.abs(x)))


def _nt_dot(a, b):
    return lax.dot_general(a, b, (((1,), (1,)), ((), ())), preferred_element_type=F32)


def _dot(a, b):
    return jnp.dot(a, b, preferred_element_type=F32)


def _mod_kernel(c_ref, w_ref, b_ref, o_ref):
    cv = c_ref[...]
    act = cv * _sigmoid(cv)
    o_ref[...] = jnp.dot(act, w_ref[...], precision=HIGHEST, preferred_element_type=F32) + b_ref[...]


def _modulation(cvecs, w_ada, b_ada):
    n_col = 512
    return pl.pallas_call(
        _mod_kernel,
        grid=(3 * D_MODEL // n_col,),
        in_specs=[
            pl.BlockSpec((8, D_MODEL), lambda j: (0, 0)),
            pl.BlockSpec((D_MODEL, n_col), lambda j: (0, j)),
            pl.BlockSpec((1, n_col), lambda j: (0, j)),
        ],
        out_specs=pl.BlockSpec((8, n_col), lambda j: (0, j)),
        out_shape=jax.ShapeDtypeStruct((8, 3 * D_MODEL), F32),
        name="modulation",
    )(cvecs, w_ada, b_ada)


def _chunk_scan(x, op, reverse):
    n = x.shape[1]
    lane = lax.broadcasted_iota(jnp.int32, x.shape, 1) & (CHUNK - 1)
    fill = 0.0 if op is jnp.add else MASKED
    k = 1
    while k < CHUNK:
        if reverse:
            shifted = pltpu.roll(x, n - k, axis=1)
            valid = lane < CHUNK - k
        else:
            shifted = pltpu.roll(x, k, axis=1)
            valid = lane >= k
        x = op(x, jnp.where(valid, shifted, fill))
        k *= 2
    return x


def _gate_scalars(g):
    lf = _log_sigmoid(g)
    pre = _chunk_scan(lf, jnp.add, reverse=False)
    suf = _chunk_scan(lf, jnp.add, reverse=True)
    total = pre + suf - lf
    fg = [slice(d * 8 + H_A, d * 8 + 2 * H_A) for d in range(2)]
    cum = jnp.concatenate([pre[fg[0]], suf[fg[1]]], axis=0)
    bend = jnp.concatenate([total[fg[0]], total[fg[1]]], axis=0)
    rr = jnp.concatenate([g[0:H_A], g[8:8 + H_A]], axis=0) - cum
    fmax = _chunk_scan(rr, jnp.maximum, reverse=False)
    bmax = _chunk_scan(rr, jnp.maximum, reverse=True)
    mloc = bend + jnp.maximum(fmax, bmax)
    pad = jnp.zeros((H_A, g.shape[1]), F32)
    out = []
    for d in range(2):
        rows = slice(d * H_A, (d + 1) * H_A)
        for q in (cum, rr, fmax if d == 0 else bmax, bend, mloc):
            out += [q[rows], pad]
    return jnp.concatenate(out, axis=0)


def _inproj_kernel(x_ref, mod_ref, ng_ref, w_ref, wift_ref, bift_ref, gsgu_ref, ws_ref, bs_ref,
                   wpb_ref, qt_ref, k_ref, vt_ref, oz_ref, sga_ref, pb_ref, gs_ref, yb_scr, *, tm):
    x = x_ref[...]
    xn = x * lax.rsqrt(jnp.mean(x * x, axis=-1, keepdims=True) + EPS) * ng_ref[...]
    shift = mod_ref[0, 0:1, :]
    scale = mod_ref[0, 1:2, :]
    hb = (xn * (1.0 + scale) + shift).astype(BF16)

    def proj(j):
        return _dot(hb, w_ref[j])

    qt_ref[...] = (_nt_dot(w_ref[P_Q], hb) * (DH_A ** -0.5)).astype(BF16)
    vt_ref[...] = _nt_dot(w_ref[P_V], hb).astype(BF16)
    k_ref[...] = proj(P_K).astype(BF16)
    za = proj(P_ZA)
    oz_ref[...] = (_sigmoid(proj(P_O)) * (za * _sigmoid(za))).astype(BF16)
    gs_ref[...] = _gate_scalars(_nt_dot(wift_ref[...], hb) + bift_ref[...])

    vb = proj(P_VB)
    vn = (vb * lax.rsqrt(jnp.mean(vb * vb, axis=-1, keepdims=True) + EPS) * gsgu_ref[...]).astype(BF16)
    u = proj(P_U)
    zb = proj(P_ZB)
    uz = u * (zb * _sigmoid(zb))
    for c in range(tm // CHUNK):
        rows = slice(c * CHUNK, (c + 1) * CHUNK)
        for g in range(G_B):
            cols = slice(g * DG_B, (g + 1) * DG_B)
            s = _dot(ws_ref[g], vn[rows, cols]) + bs_ref[:, cols]
            yb_scr[rows, cols] = (uz[rows, cols] * s).astype(BF16)
    pb = _sigmoid(proj(P_GB)) * _dot(yb_scr[...], wpb_ref[...])
    pb_ref[...] = pb.astype(BF16)
    sga_ref[...] = _sigmoid(proj(P_GA)).astype(BF16)


def _inproj(x2d, mod3, seq_tiles, ng, w10, wift, bift, gsgu, ws, bs_full, wpb, *, tm):
    n_tok = x2d.shape[0]
    const2 = lambda i: (0, 0)
    const3 = lambda i: (0, 0, 0)
    resident = dict(pipeline_mode=pl.Buffered(1))
    tok_spec = pl.BlockSpec((tm, D_MODEL), lambda i: (i, 0))
    feat_spec = pl.BlockSpec((D_MODEL, tm), lambda i: (0, i))
    if mod3.shape[0] == 1:
        mod_map = const3
    else:
        mod_map = lambda i: (i // seq_tiles, 0, 0)
    tok_out = jax.ShapeDtypeStruct((n_tok, D_MODEL), BF16)
    feat_out = jax.ShapeDtypeStruct((D_MODEL, n_tok), BF16)
    return pl.pallas_call(
        functools.partial(_inproj_kernel, tm=tm),
        grid=(n_tok // tm,),
        in_specs=[
            tok_spec,
            pl.BlockSpec((1, 3, D_MODEL), mod_map),
            pl.BlockSpec((1, D_MODEL), const2),
            pl.BlockSpec((N_PROJ, D_MODEL, D_MODEL), const3, **resident),
            pl.BlockSpec((N_GATES, D_MODEL), const2),
            pl.BlockSpec((N_GATES, 1), const2),
            pl.BlockSpec((1, D_MODEL), const2),
            pl.BlockSpec((G_B, CHUNK, CHUNK), const3),
            pl.BlockSpec((CHUNK, D_MODEL), const2),
            pl.BlockSpec((D_MODEL, D_MODEL), const2, **resident),
        ],
        out_specs=[feat_spec, tok_spec, feat_spec, tok_spec, tok_spec, tok_spec,
                   pl.BlockSpec((GS_ROWS, tm), lambda i: (0, i))],
        out_shape=[feat_out, tok_out, feat_out, tok_out, tok_out, tok_out,
                   jax.ShapeDtypeStruct((GS_ROWS, n_tok), F32)],
        scratch_shapes=[pltpu.VMEM((tm, D_MODEL), BF16)],
        compiler_params=pltpu.CompilerParams(
            dimension_semantics=("arbitrary",), vmem_limit_bytes=VMEM_LIMIT_BYTES),
        name="inproj",
    )(x2d, mod3, ng, w10, wift, bift, gsgu, ws, bs_full, wpb)


def _store_state_bf16(cn_scr, d, h, c_val, n_val):
    cn_scr[d, h, 0:DH_A, :] = c_val.astype(BF16)
    cn_scr[d, h, DH_A:CN_ROWS, :] = jnp.broadcast_to(n_val, (BF16_ROWS, DH_A)).astype(BF16)


def _mlstm_sweep(d, tblk, qt_ref, k_ref, vt_ref, gs_ref, c_scr, n_scr, m_scr, cn_scr, emit):
    n_chunks = tblk // CHUNK
    ri = lax.broadcasted_iota(jnp.int32, (CHUNK, CHUNK), 0)
    ci = lax.broadcasted_iota(jnp.int32, (CHUNK, CHUNK), 1)
    mask_st = (ri <= ci) if d == 0 else (ri >= ci)

    def body(i, carry):
        c = i if d == 0 else n_chunks - 1 - i
        r0 = pl.multiple_of(c * CHUNK, CHUNK)

        def gate_scalar(q):
            g0 = d * GS_PER_DIR + q * 8
            return gs_ref[g0:g0 + H_A, pl.ds(r0, CHUNK)]

        cum = gate_scalar(GS_CUM)
        rr = gate_scalar(GS_RR)
        b_end = gate_scalar(GS_BEND)
        m_prev = m_scr[d]
        a = cum + m_prev
        m_t = jnp.maximum(a, cum + gate_scalar(GS_CMAX))
        u = cum - m_t
        sc = jnp.exp(a - m_t)
        em = jnp.exp(-m_t)
        m_new = jnp.maximum(b_end + m_prev, gate_scalar(GS_MLOC))
        e_row = jnp.exp(b_end + rr - m_new)
        sp = jnp.exp(b_end + m_prev - m_new)[:, 0:1]
        m_scr[d] = m_new

        for h in range(H_A):
            hs = slice(h * DH_A, (h + 1) * DH_A)
            row = slice(h, h + 1)
            qt = qt_ref[hs, pl.ds(r0, CHUNK)]
            kh = k_ref[pl.ds(r0, CHUNK), hs]
            vt = vt_ref[hs, pl.ds(r0, CHUNK)]

            rr_st = jnp.broadcast_to(rr[row], (CHUNK, CHUNK)).T
            decay = jnp.exp(jnp.where(mask_st, u[row] + rr_st, MASKED))
            s_t = _dot(kh, qt) * decay
            cq = _dot(cn_scr[d, h], qt)
            den = jnp.sum(s_t, axis=0, keepdims=True) + sc[row] * cq[DH_A:DH_A + 1]
            inv = 1.0 / jnp.maximum(jnp.abs(den), em[row])
            num = _dot(vt, s_t.astype(BF16)) + sc[row] * cq[0:DH_A]
            emit(c, h, num * inv)

            vte = (vt.astype(F32) * e_row[row]).astype(BF16)
            lhs = jnp.concatenate([vte, jnp.broadcast_to(e_row[row], (BF16_ROWS, CHUNK)).astype(BF16)], axis=0)
            upd = _dot(lhs, kh)
            c_new = sp[row] * c_scr[d, h] + upd[0:DH_A]
            n_new = sp[row] * n_scr[d, row, :] + upd[DH_A:DH_A + 1]
            c_scr[d, h] = c_new
            n_scr[d, row, :] = n_new
            _store_state_bf16(cn_scr, d, h, c_new, n_new)
        return carry

    lax.fori_loop(0, n_chunks, body, 0)


def _mixer_kernel(*refs, nblk, tblk, has_init, emit_state):
    refs = list(refs)
    qt_ref, k_ref, vt_ref, oz_ref, sga_ref, pb_ref, gs_ref, x_ref, mod_ref = refs[:9]
    refs = refs[9:]
    if has_init:
        c0_ref, n0_ref, m0_ref = refs[:3]
        refs = refs[3:]
    wpa_ref, wout_ref, nf_ref, y_ref = refs[:4]
    refs = refs[4:]
    if emit_state:
        cf_ref, nfin_ref, mfin_ref = refs[:3]
        refs = refs[3:]
    c_scr, n_scr, m_scr, cn_scr, hb_scr, ya_scr = refs

    n_chunks = tblk // CHUNK
    step = pl.program_id(1)
    sweep = functools.partial(_mlstm_sweep, tblk=tblk, qt_ref=qt_ref, k_ref=k_ref, vt_ref=vt_ref, gs_ref=gs_ref,
                              c_scr=c_scr, n_scr=n_scr, m_scr=m_scr, cn_scr=cn_scr)

    @pl.when(step == 0)
    def _():
        if has_init:
            c_scr[...] = c0_ref[0]
            n_scr[...] = n0_ref[0]
            m_scr[...] = m0_ref[0]
        else:
            c_scr[...] = jnp.zeros_like(c_scr)
            n_scr[...] = jnp.zeros_like(n_scr)
            m_scr[...] = jnp.zeros_like(m_scr)
        for d in range(2):
            for h in range(H_A):
                _store_state_bf16(cn_scr, d, h, c_scr[d, h], n_scr[d, h:h + 1, :])

    @pl.when(step < nblk)
    def _():
        blk = nblk - 1 - step

        def emit(c, h, h_t):
            hb_scr[blk * n_chunks + c, h * DH_A:(h + 1) * DH_A, :] = h_t.astype(BF16)

        sweep(1, emit=emit)

    @pl.when(step >= nblk)
    def _():
        blk = step - nblk

        def emit(c, h, h_t):
            hs = slice(h * DH_A, (h + 1) * DH_A)
            rows = pl.ds(pl.multiple_of(c * CHUNK, CHUNK), CHUNK)
            h_a = (h_t + hb_scr[blk * n_chunks + c, hs, :].astype(F32)).T
            ya_scr[rows, hs] = (oz_ref[rows, hs].astype(F32) * h_a).astype(BF16)

        sweep(0, emit=emit)

        pa = _dot(ya_scr[...], wpa_ref[...])
        merged = sga_ref[...].astype(F32) * pa + pb_ref[...].astype(F32)
        upd = _dot(merged.astype(BF16), wout_ref[...])
        out = x_ref[...] + mod_ref[0, 2:3, :] * upd
        y_ref[...] = out * lax.rsqrt(jnp.mean(out * out, axis=-1, keepdims=True) + EPS) * nf_ref[...]

    if emit_state:
        @pl.when(step == 2 * nblk - 1)
        def _():
            cf_ref[0] = c_scr[...]
            nfin_ref[0] = n_scr[...]
            mfin_ref[0] = m_scr[...]


def _mixer(qt, k, vt, oz, sga, pb, gs, x2d, mod3, init_state, wpa, wout, norm_f, *, n_seq, seq_len, tblk,
           emit_state):
    nblk = seq_len // tblk
    has_init = init_state is not None

    def sweep_blk(b, s):
        return b * nblk + jnp.where(s < nblk, nblk - 1 - s, s - nblk)

    def tail_blk(b, s):
        return b * nblk + jnp.maximum(s - nblk, 0)

    sweep_tok = pl.BlockSpec((tblk, D_MODEL), lambda b, s: (sweep_blk(b, s), 0))
    sweep_feat = pl.BlockSpec((D_MODEL, tblk), lambda b, s: (0, sweep_blk(b, s)))
    tail_spec = pl.BlockSpec((tblk, D_MODEL), lambda b, s: (tail_blk(b, s), 0))
    const2 = lambda b, s: (0, 0)
    if mod3.shape[0] == 1:
        mod_map = lambda b, s: (0, 0, 0)
    else:
        mod_map = lambda b, s: (b, 0, 0)
    c_spec = pl.BlockSpec((1, 2, H_A, DH_A, DH_A), lambda b, s: (b, 0, 0, 0, 0))
    n_spec = pl.BlockSpec((1, 2, H_A, DH_A), lambda b, s: (b, 0, 0, 0))
    m_spec = pl.BlockSpec((1, 2, H_A, CHUNK), lambda b, s: (b, 0, 0, 0))

    in_specs = [
        sweep_feat, sweep_tok, sweep_feat,
        tail_spec, tail_spec, tail_spec,
        pl.BlockSpec((GS_ROWS, tblk), lambda b, s: (0, sweep_blk(b, s))),
        tail_spec,
        pl.BlockSpec((1, 3, D_MODEL), mod_map),
    ]
    args = [qt, k, vt, oz, sga, pb, gs, x2d, mod3]
    if has_init:
        in_specs += [c_spec, n_spec, m_spec]
        args += list(init_state)
    in_specs += [
        pl.BlockSpec((D_MODEL, D_MODEL), const2),
        pl.BlockSpec((D_MODEL, D_MODEL), const2),
        pl.BlockSpec((1, D_MODEL), const2),
    ]
    args += [wpa, wout, norm_f]

    out_specs = [tail_spec]
    out_shape = [jax.ShapeDtypeStruct((n_seq * seq_len, D_MODEL), F32)]
    if emit_state:
        out_specs += [c_spec, n_spec, m_spec]
        out_shape += [
            jax.ShapeDtypeStruct((n_seq, 2, H_A, DH_A, DH_A), F32),
            jax.ShapeDtypeStruct((n_seq, 2, H_A, DH_A), F32),
            jax.ShapeDtypeStruct((n_seq, 2, H_A, CHUNK), F32),
        ]

    return pl.pallas_call(
        functools.partial(_mixer_kernel, nblk=nblk, tblk=tblk, has_init=has_init, emit_state=emit_state),
        grid=(n_seq, 2 * nblk),
        in_specs=in_specs,
        out_specs=out_specs,
        out_shape=out_shape,
        scratch_shapes=[
            pltpu.VMEM((2, H_A, DH_A, DH_A), F32),
            pltpu.VMEM((2, H_A, DH_A), F32),
            pltpu.VMEM((2, H_A, CHUNK), F32),
            pltpu.VMEM((2, H_A, CN_ROWS, DH_A), BF16),
            pltpu.VMEM((seq_len // CHUNK, D_MODEL, CHUNK), BF16),
            pltpu.VMEM((tblk, D_MODEL), BF16),
        ],
        compiler_params=pltpu.CompilerParams(
            dimension_semantics=("arbitrary", "arbitrary"), vmem_limit_bytes=VMEM_LIMIT_BYTES),
        name="mixer",
    )(*args)


def _split_w_in(w):
    gate0 = 5 * D_MODEL
    rest0 = gate0 + N_GATES
    parts = [w[:, j * D_MODEL:(j + 1) * D_MODEL] for j in range(5)]
    parts += [w[:, rest0 + j * D_MODEL:rest0 + (j + 1) * D_MODEL] for j in range(5)]
    parts[P_Q] = parts[P_Q].T
    parts[P_V] = parts[P_V].T
    w10 = jnp.stack(parts, axis=0).astype(BF16)
    return w10, w[:, gate0:rest0].T.astype(BF16)


def kernel(x_prompt, x_sample, c, state_C, state_n, state_m, c_ctx, norm_g, w_ada, b_ada, w_in, b_if, g_sgu, w_s, b_s,
           w_proj_a, w_proj_b, w_out, norm_f):
    n_ctx, t_ctx, _ = x_prompt.shape
    n_lat, t_lat, _ = x_sample.shape
    depth = norm_g.shape[0]
    assert depth == 1, "single-layer step"
    l = 0

    cvecs = jnp.concatenate([c, c_ctx[None], jnp.zeros((8 - n_lat - 1, D_MODEL), F32)], axis=0)
    mod = _modulation(cvecs, w_ada[l], b_ada[l][None]).reshape(8, 3, D_MODEL)
    mod_lat = mod[:n_lat]
    mod_ctx = mod[n_lat:n_lat + 1]

    w10, wift = _split_w_in(w_in[l])
    bift = b_if[l].reshape(N_GATES, 1)
    ng = norm_g[l][None]
    gsgu = g_sgu[l][None]
    ws = w_s[l].astype(BF16)
    bs_full = jnp.repeat(jnp.transpose(b_s[l]), DG_B, axis=1)
    wpa = w_proj_a[l].astype(BF16)
    wpb = w_proj_b[l].astype(BF16)
    wout = w_out[l].astype(BF16)
    nf = norm_f[None]

    tm = 512
    xp2 = x_prompt.reshape(n_ctx * t_ctx, D_MODEL)
    xs2 = x_sample.reshape(n_lat * t_lat, D_MODEL)
    shared = (ng, w10, wift, bift, gsgu, ws, bs_full, wpb)
    pc = _inproj(xp2, mod_ctx, 1, *shared, tm=tm)
    ps = _inproj(xs2, mod_lat, t_lat // tm, *shared, tm=tm)

    y_p, c_new, n_new, m_new = _mixer(*pc, xp2, mod_ctx, None, wpa, wout, nf, n_seq=n_ctx, seq_len=t_ctx,
                                      tblk=t_ctx, emit_state=True)
    init = (state_C[:, l], state_n[:, l],
            jnp.broadcast_to(state_m[:, l][..., None], (n_lat, 2, H_A, CHUNK)))
    (y_s,) = _mixer(*ps, xs2, mod_lat, init, wpa, wout, nf, n_seq=n_lat, seq_len=t_lat, tblk=512,
                    emit_state=False)

    return (y_p.reshape(n_ctx, t_ctx, D_MODEL), y_s.reshape(n_lat, t_lat, D_MODEL),
            c_new[:, None], n_new[:, None], m_new[:, None, :, :, 0])
```

```python
import functools

import jax
import jax.numpy as jnp
from jax import lax
from jax.experimental import pallas as pl
from jax.experimental.pallas import tpu as pltpu

D_MODEL = 1024
H_A = 4
DH_A = 256
G_B = 4
DG_B = 256
CHUNK = 128
SCAN = 256
N_GATES = 16
EPS = 1e-6
MASKED = -1e30
F32 = jnp.float32
BF16 = jnp.bfloat16
HIGHEST = lax.Precision.HIGHEST
VMEM_LIMIT_BYTES = 60000 * 1024
BF16_ROWS = 16
CN_ROWS = DH_A + BF16_ROWS
GS_PER_DIR = 40
GS_ROWS = 2 * GS_PER_DIR
GS_CUM, GS_RR, GS_CMAX, GS_BEND, GS_MLOC = range(5)

N_PROJ = 10
P_Q, P_K, P_V, P_O, P_ZA, P_U, P_VB, P_ZB, P_GA, P_GB = range(N_PROJ)


def _sigmoid(x):
    return 1.0 / (1.0 + jnp.exp(-x))


def _log_sigmoid(x):
    return jnp.minimum(x, 0.0) - jnp.log1p(jnp.exp(-jnp.abs(x)))


def _nt_dot(a, b):
    return lax.dot_general(a, b, (((1,), (1,)), ((), ())), preferred_element_type=F32)


def _dot(a, b):
    return jnp.dot(a, b, preferred_element_type=F32)


def _mod_kernel(c_ref, w_ref, b_ref, o_ref):
    cv = c_ref[...]
    act = cv * _sigmoid(cv)
    o_ref[...] = jnp.dot(act, w_ref[...], precision=HIGHEST, preferred_element_type=F32) + b_ref[...]


def _modulation(cvecs, w_ada, b_ada):
    n_col = 512
    return pl.pallas_call(
        _mod_kernel,
        grid=(3 * D_MODEL // n_col,),
        in_specs=[
            pl.BlockSpec((8, D_MODEL), lambda j: (0, 0)),
            pl.BlockSpec((D_MODEL, n_col), lambda j: (0, j)),
            pl.BlockSpec((1, n_col), lambda j: (0, j)),
        ],
        out_specs=pl.BlockSpec((8, n_col), lambda j: (0, j)),
        out_shape=jax.ShapeDtypeStruct((8, 3 * D_MODEL), F32),
        name="modulation",
    )(cvecs, w_ada, b_ada)


def _segment_scan(x, op, reverse):
    n = x.shape[1]
    lane = lax.broadcasted_iota(jnp.int32, x.shape, 1) & (SCAN - 1)
    fill = 0.0 if op is jnp.add else MASKED
    k = 1
    while k < SCAN:
        if reverse:
            shifted = pltpu.roll(x, n - k, axis=1)
            valid = lane < SCAN - k
        else:
            shifted = pltpu.roll(x, k, axis=1)
            valid = lane >= k
        x = op(x, jnp.where(valid, shifted, fill))
        k *= 2
    return x


def _gate_scalars(g):
    lf = _log_sigmoid(g)
    pre = _segment_scan(lf, jnp.add, reverse=False)
    suf = _segment_scan(lf, jnp.add, reverse=True)
    total = pre + suf - lf
    fg = [slice(d * 8 + H_A, d * 8 + 2 * H_A) for d in range(2)]
    cum = jnp.concatenate([pre[fg[0]], suf[fg[1]]], axis=0)
    bend = jnp.concatenate([total[fg[0]], total[fg[1]]], axis=0)
    rr = jnp.concatenate([g[0:H_A], g[8:8 + H_A]], axis=0) - cum
    fmax = _segment_scan(rr, jnp.maximum, reverse=False)
    bmax = _segment_scan(rr, jnp.maximum, reverse=True)
    mloc = bend + jnp.maximum(fmax, bmax)
    pad = jnp.zeros((H_A, g.shape[1]), F32)
    out = []
    for d in range(2):
        rows = slice(d * H_A, (d + 1) * H_A)
        for q in (cum, rr, fmax if d == 0 else bmax, bend, mloc):
            out += [q[rows], pad]
    return jnp.concatenate(out, axis=0)


def _inproj_kernel(x_ref, mod_ref, ng_ref, wa_ref, wb_ref, wqvt_ref, wift_ref, bift_ref, gsgu_ref, ws_ref, bs_ref,
                   wpb_ref, qt_ref, k_ref, vt_ref, oz_ref, sga_ref, pb_ref, gs_ref, yb_scr, *, tm):
    x = x_ref[...]
    xn = x * lax.rsqrt(jnp.mean(x * x, axis=-1, keepdims=True) + EPS) * ng_ref[...]
    shift = mod_ref[0, 0:1, :]
    scale = mod_ref[0, 1:2, :]
    hb = (xn * (1.0 + scale) + shift).astype(BF16)

    def proj(j):
        w_ref, col = (wa_ref, j) if j < N_PROJ // 2 else (wb_ref, j - N_PROJ // 2)
        return _dot(hb, w_ref[:, col * D_MODEL:(col + 1) * D_MODEL])

    qt_ref[...] = (_nt_dot(wqvt_ref[0], hb) * (DH_A ** -0.5)).astype(BF16)
    vt_ref[...] = _nt_dot(wqvt_ref[1], hb).astype(BF16)
    k_ref[...] = proj(P_K).astype(BF16)
    za = proj(P_ZA)
    oz_ref[...] = (_sigmoid(proj(P_O)) * (za * _sigmoid(za))).astype(BF16)
    gs_ref[...] = _gate_scalars(_nt_dot(wift_ref[...], hb) + bift_ref[...])

    vb = proj(P_VB)
    vn = (vb * lax.rsqrt(jnp.mean(vb * vb, axis=-1, keepdims=True) + EPS) * gsgu_ref[...]).astype(BF16)
    u = proj(P_U)
    zb = proj(P_ZB)
    uz = u * (zb * _sigmoid(zb))
    for c in range(tm // CHUNK):
        rows = slice(c * CHUNK, (c + 1) * CHUNK)
        for g in range(G_B):
            cols = slice(g * DG_B, (g + 1) * DG_B)
            s = _dot(ws_ref[g], vn[rows, cols]) + bs_ref[:, cols]
            yb_scr[rows, cols] = (uz[rows, cols] * s).astype(BF16)
    pb = _sigmoid(proj(P_GB)) * _dot(yb_scr[...], wpb_ref[...])
    pb_ref[...] = pb.astype(BF16)
    sga_ref[...] = _sigmoid(proj(P_GA)).astype(BF16)


def _inproj(x2d, mod3, seq_tiles, ng, wa, wb, wqvt, wift, bift, gsgu, ws, bs_full, wpb, *, tm):
    n_tok = x2d.shape[0]
    const2 = lambda i: (0, 0)
    const3 = lambda i: (0, 0, 0)
    resident = dict(pipeline_mode=pl.Buffered(1))
    tok_spec = pl.BlockSpec((tm, D_MODEL), lambda i: (i, 0))
    feat_spec = pl.BlockSpec((D_MODEL, tm), lambda i: (0, i))
    if mod3.shape[0] == 1:
        mod_map = const3
    else:
        mod_map = lambda i: (i // seq_tiles, 0, 0)
    tok_out = jax.ShapeDtypeStruct((n_tok, D_MODEL), BF16)
    feat_out = jax.ShapeDtypeStruct((D_MODEL, n_tok), BF16)
    return pl.pallas_call(
        functools.partial(_inproj_kernel, tm=tm),
        grid=(n_tok // tm,),
        in_specs=[
            tok_spec,
            pl.BlockSpec((1, 3, D_MODEL), mod_map),
            pl.BlockSpec((1, D_MODEL), const2),
            pl.BlockSpec((D_MODEL, N_PROJ // 2 * D_MODEL), const2, **resident),
            pl.BlockSpec((D_MODEL, N_PROJ // 2 * D_MODEL), const2, **resident),
            pl.BlockSpec((2, D_MODEL, D_MODEL), const3, **resident),
            pl.BlockSpec((N_GATES, D_MODEL), const2),
            pl.BlockSpec((N_GATES, 1), const2),
            pl.BlockSpec((1, D_MODEL), const2),
            pl.BlockSpec((G_B, CHUNK, CHUNK), const3),
            pl.BlockSpec((CHUNK, D_MODEL), const2),
            pl.BlockSpec((D_MODEL, D_MODEL), const2, **resident),
        ],
        out_specs=[feat_spec, tok_spec, feat_spec, tok_spec, tok_spec, tok_spec,
                   pl.BlockSpec((GS_ROWS, tm), lambda i: (0, i))],
        out_shape=[feat_out, tok_out, feat_out, tok_out, tok_out, tok_out,
                   jax.ShapeDtypeStruct((GS_ROWS, n_tok), F32)],
        scratch_shapes=[pltpu.VMEM((tm, D_MODEL), BF16)],
        compiler_params=pltpu.CompilerParams(
            dimension_semantics=("arbitrary",), vmem_limit_bytes=VMEM_LIMIT_BYTES),
        name="inproj",
    )(x2d, mod3, ng, wa, wb, wqvt, wift, bift, gsgu, ws, bs_full, wpb)


def _store_state_bf16(cn_scr, d, h, c_val, n_val):
    cn_scr[d, h, 0:DH_A, :] = c_val.astype(BF16)
    cn_scr[d, h, DH_A:CN_ROWS, :] = jnp.broadcast_to(n_val, (BF16_ROWS, DH_A)).astype(BF16)


def _mlstm_sweep(d, tblk, qt_ref, k_ref, vt_ref, gs_ref, c_scr, n_scr, m_scr, cn_scr, emit):
    n_seg = tblk // SCAN
    ri = lax.broadcasted_iota(jnp.int32, (SCAN, SCAN), 0)
    ci = lax.broadcasted_iota(jnp.int32, (SCAN, SCAN), 1)
    mask_st = (ri <= ci) if d == 0 else (ri >= ci)

    for i in range(n_seg):
        c = i if d == 0 else n_seg - 1 - i
        seg = slice(c * SCAN, (c + 1) * SCAN)

        def gate_scalar(q):
            g0 = d * GS_PER_DIR + q * 8
            return gs_ref[g0:g0 + H_A, seg]

        cum = gate_scalar(GS_CUM)
        rr = gate_scalar(GS_RR)
        b_end = gate_scalar(GS_BEND)
        m_prev = m_scr[d]
        a = cum + m_prev
        m_t = jnp.maximum(a, cum + gate_scalar(GS_CMAX))
        u = cum - m_t
        sc = jnp.exp(a - m_t)
        em = jnp.exp(-m_t)
        m_new = jnp.maximum(b_end + m_prev, gate_scalar(GS_MLOC))
        e_row = jnp.exp(b_end + rr - m_new)
        sp = jnp.exp(b_end + m_prev - m_new)[:, 0:1]
        m_scr[d] = m_new

        for h in range(H_A):
            hs = slice(h * DH_A, (h + 1) * DH_A)
            row = slice(h, h + 1)
            qt = qt_ref[hs, seg]
            kh = k_ref[seg, hs]
            vt = vt_ref[hs, seg]

            rr_st = jnp.broadcast_to(rr[row], (SCAN, SCAN)).T
            decay = jnp.exp(jnp.where(mask_st, u[row] + rr_st, MASKED))
            s_t = _dot(kh, qt) * decay
            cq = _dot(cn_scr[d, h], qt)
            den = jnp.sum(s_t, axis=0, keepdims=True) + sc[row] * cq[DH_A:DH_A + 1]
            inv = 1.0 / jnp.maximum(jnp.abs(den), em[row])
            num = _dot(vt, s_t.astype(BF16)) + sc[row] * cq[0:DH_A]
            emit(c, h, num * inv)

            vte = (vt.astype(F32) * e_row[row]).astype(BF16)
            lhs = jnp.concatenate([vte, jnp.broadcast_to(e_row[row], (BF16_ROWS, SCAN)).astype(BF16)], axis=0)
            upd = _dot(lhs, kh)
            c_new = sp[row] * c_scr[d, h] + upd[0:DH_A]
            n_new = sp[row] * n_scr[d, row, :] + upd[DH_A:DH_A + 1]
            c_scr[d, h] = c_new
            n_scr[d, row, :] = n_new
            _store_state_bf16(cn_scr, d, h, c_new, n_new)


def _mixer_kernel(*refs, nblk, tblk, has_init, emit_state):
    refs = list(refs)
    qt_ref, k_ref, vt_ref, oz_ref, sga_ref, pb_ref, gs_ref, x_ref, mod_ref = refs[:9]
    refs = refs[9:]
    if has_init:
        c0_ref, n0_ref, m0_ref = refs[:3]
        refs = refs[3:]
    wpa_ref, wout_ref, nf_ref, y_ref = refs[:4]
    refs = refs[4:]
    if emit_state:
        cf_ref, nfin_ref, mfin_ref = refs[:3]
        refs = refs[3:]
    c_scr, n_scr, m_scr, cn_scr, hb_scr, ya_scr = refs

    n_seg = tblk // SCAN
    step = pl.program_id(1)
    sweep = functools.partial(_mlstm_sweep, tblk=tblk, qt_ref=qt_ref, k_ref=k_ref, vt_ref=vt_ref, gs_ref=gs_ref,
                              c_scr=c_scr, n_scr=n_scr, m_scr=m_scr, cn_scr=cn_scr)

    @pl.when(step == 0)
    def _():
        if has_init:
            c_scr[...] = c0_ref[0]
            n_scr[...] = n0_ref[0]
            m_scr[...] = m0_ref[0]
        else:
            c_scr[...] = jnp.zeros_like(c_scr)
            n_scr[...] = jnp.zeros_like(n_scr)
            m_scr[...] = jnp.zeros_like(m_scr)
        for d in range(2):
            for h in range(H_A):
                _store_state_bf16(cn_scr, d, h, c_scr[d, h], n_scr[d, h:h + 1, :])

    @pl.when(step < nblk)
    def _():
        blk = nblk - 1 - step

        def emit(c, h, h_t):
            hb_scr[blk * n_seg + c, h * DH_A:(h + 1) * DH_A, :] = h_t.astype(BF16)

        sweep(1, emit=emit)

    @pl.when(step >= nblk)
    def _():
        blk = step - nblk

        def emit(c, h, h_t):
            hs = slice(h * DH_A, (h + 1) * DH_A)
            rows = slice(c * SCAN, (c + 1) * SCAN)
            h_a = (h_t + hb_scr[blk * n_seg + c, hs, :].astype(F32)).T
            ya_scr[rows, hs] = (oz_ref[rows, hs].astype(F32) * h_a).astype(BF16)

        sweep(0, emit=emit)

        pa = _dot(ya_scr[...], wpa_ref[...])
        merged = sga_ref[...].astype(F32) * pa + pb_ref[...].astype(F32)
        upd = _dot(merged.astype(BF16), wout_ref[...])
        out = x_ref[...] + mod_ref[0, 2:3, :] * upd
        y_ref[...] = out * lax.rsqrt(jnp.mean(out * out, axis=-1, keepdims=True) + EPS) * nf_ref[...]

    if emit_state:
        @pl.when(step == 2 * nblk - 1)
        def _():
            cf_ref[0] = c_scr[...]
            nfin_ref[0] = n_scr[...]
            mfin_ref[0] = m_scr[...]


def _mixer(qt, k, vt, oz, sga, pb, gs, x2d, mod3, init_state, wpa, wout, norm_f, *, n_seq, seq_len, tblk,
           emit_state):
    nblk = seq_len // tblk
    has_init = init_state is not None

    def sweep_blk(b, s):
        return b * nblk + jnp.where(s < nblk, nblk - 1 - s, s - nblk)

    def tail_blk(b, s):
        return b * nblk + jnp.maximum(s - nblk, 0)

    sweep_tok = pl.BlockSpec((tblk, D_MODEL), lambda b, s: (sweep_blk(b, s), 0))
    sweep_feat = pl.BlockSpec((D_MODEL, tblk), lambda b, s: (0, sweep_blk(b, s)))
    tail_spec = pl.BlockSpec((tblk, D_MODEL), lambda b, s: (tail_blk(b, s), 0))
    const2 = lambda b, s: (0, 0)
    if mod3.shape[0] == 1:
        mod_map = lambda b, s: (0, 0, 0)
    else:
        mod_map = lambda b, s: (b, 0, 0)
    c_spec = pl.BlockSpec((1, 2, H_A, DH_A, DH_A), lambda b, s: (b, 0, 0, 0, 0))
    n_spec = pl.BlockSpec((1, 2, H_A, DH_A), lambda b, s: (b, 0, 0, 0))
    m_spec = pl.BlockSpec((1, 2, H_A, SCAN), lambda b, s: (b, 0, 0, 0))

    in_specs = [
        sweep_feat, sweep_tok, sweep_feat,
        tail_spec, tail_spec, tail_spec,
        pl.BlockSpec((GS_ROWS, tblk), lambda b, s: (0, sweep_blk(b, s))),
        tail_spec,
        pl.BlockSpec((1, 3, D_MODEL), mod_map),
    ]
    args = [qt, k, vt, oz, sga, pb, gs, x2d, mod3]
    if has_init:
        in_specs += [c_spec, n_spec, m_spec]
        args += list(init_state)
    in_specs += [
        pl.BlockSpec((D_MODEL, D_MODEL), const2),
        pl.BlockSpec((D_MODEL, D_MODEL), const2),
        pl.BlockSpec((1, D_MODEL), const2),
    ]
    args += [wpa, wout, norm_f]

    out_specs = [tail_spec]
    out_shape = [jax.ShapeDtypeStruct((n_seq * seq_len, D_MODEL), F32)]
    if emit_state:
        out_specs += [c_spec, n_spec, m_spec]
        out_shape += [
            jax.ShapeDtypeStruct((n_seq, 2, H_A, DH_A, DH_A), F32),
            jax.ShapeDtypeStruct((n_seq, 2, H_A, DH_A), F32),
            jax.ShapeDtypeStruct((n_seq, 2, H_A, SCAN), F32),
        ]

    return pl.pallas_call(
        functools.partial(_mixer_kernel, nblk=nblk, tblk=tblk, has_init=has_init, emit_state=emit_state),
        grid=(n_seq, 2 * nblk),
        in_specs=in_specs,
        out_specs=out_specs,
        out_shape=out_shape,
        scratch_shapes=[
            pltpu.VMEM((2, H_A, DH_A, DH_A), F32),
            pltpu.VMEM((2, H_A, DH_A), F32),
            pltpu.VMEM((2, H_A, SCAN), F32),
            pltpu.VMEM((2, H_A, CN_ROWS, DH_A), BF16),
            pltpu.VMEM((seq_len // SCAN, D_MODEL, SCAN), BF16),
            pltpu.VMEM((tblk, D_MODEL), BF16),
        ],
        compiler_params=pltpu.CompilerParams(
            dimension_semantics=("arbitrary", "arbitrary"), vmem_limit_bytes=VMEM_LIMIT_BYTES),
        name="mixer",
    )(*args)


def _split_w_in(w):
    gate0 = N_PROJ // 2 * D_MODEL
    rest0 = gate0 + N_GATES
    wa = w[:, :gate0].astype(BF16)
    wb = w[:, rest0:].astype(BF16)
    wqvt = jnp.stack([w[:, P_Q * D_MODEL:(P_Q + 1) * D_MODEL].T, w[:, P_V * D_MODEL:(P_V + 1) * D_MODEL].T],
                     axis=0).astype(BF16)
    return wa, wb, wqvt, w[:, gate0:rest0].T.astype(BF16)


def kernel(x_prompt, x_sample, c, state_C, state_n, state_m, c_ctx, norm_g, w_ada, b_ada, w_in, b_if, g_sgu, w_s, b_s,
           w_proj_a, w_proj_b, w_out, norm_f):
    n_ctx, t_ctx, _ = x_prompt.shape
    n_lat, t_lat, _ = x_sample.shape
    depth = norm_g.shape[0]
    assert depth == 1, "single-layer step"
    l = 0

    cvecs = jnp.concatenate([c, c_ctx[None], jnp.zeros((8 - n_lat - 1, D_MODEL), F32)], axis=0)
    mod = _modulation(cvecs, w_ada[l], b_ada[l][None]).reshape(8, 3, D_MODEL)
    mod_lat = mod[:n_lat]
    mod_ctx = mod[n_lat:n_lat + 1]

    wa, wb, wqvt, wift = _split_w_in(w_in[l])
    bift = b_if[l].reshape(N_GATES, 1)
    ng = norm_g[l][None]
    gsgu = g_sgu[l][None]
    ws = w_s[l].astype(BF16)
    bs_full = jnp.repeat(jnp.transpose(b_s[l]), DG_B, axis=1)
    wpa = w_proj_a[l].astype(BF16)
    wpb = w_proj_b[l].astype(BF16)
    wout = w_out[l].astype(BF16)
    nf = norm_f[None]

    tm = 512
    xp2 = x_prompt.reshape(n_ctx * t_ctx, D_MODEL)
    xs2 = x_sample.reshape(n_lat * t_lat, D_MODEL)
    shared = (ng, wa, wb, wqvt, wift, bift, gsgu, ws, bs_full, wpb)
    pc = _inproj(xp2, mod_ctx, 1, *shared, tm=tm)
    ps = _inproj(xs2, mod_lat, t_lat // tm, *shared, tm=tm)

    y_p, c_new, n_new, m_new = _mixer(*pc, xp2, mod_ctx, None, wpa, wout, nf, n_seq=n_ctx, seq_len=t_ctx,
                                      tblk=t_ctx, emit_state=True)
    init = (state_C[:, l], state_n[:, l],
            jnp.broadcast_to(state_m[:, l][..., None], (n_lat, 2, H_A, SCAN)))
    (y_s,) = _mixer(*ps, xs2, mod_lat, init, wpa, wout, nf, n_seq=n_lat, seq_len=t_lat, tblk=512,
                    emit_state=False)

    return (y_p.reshape(n_ctx, t_ctx, D_MODEL), y_s.reshape(n_lat, t_lat, D_MODEL),
            c_new[:, None], n_new[:, None], m_new[:, None, :, :, 0])
```

```python
import functools

import jax
import jax.numpy as jnp
from jax import lax
from jax.experimental import pallas as pl
from jax.experimental.pallas import tpu as pltpu

D_MODEL = 1024
H_A = 4
DH_A = 256
G_B = 4
DG_B = 256
CHUNK = 128
SCAN = 256
N_GATES = 16
GATE_LANES = 128
EPS = 1e-6
MASKED = -1e30
F32 = jnp.float32
BF16 = jnp.bfloat16
HIGHEST = lax.Precision.HIGHEST
VMEM_LIMIT_BYTES = 60000 * 1024
BF16_ROWS = 16
CN_ROWS = DH_A + BF16_ROWS
GS_PER_DIR = 40
GS_ROWS = 2 * GS_PER_DIR
GS_CUM, GS_RR, GS_CMAX, GS_BEND, GS_MLOC = range(5)

N_PROJ = 10
P_Q, P_K, P_V, P_O, P_ZA, P_U, P_VB, P_ZB, P_GA, P_GB = range(N_PROJ)


def _sigmoid(x):
    return 1.0 / (1.0 + jnp.exp(-x))


def _log_sigmoid(x):
    return jnp.minimum(x, 0.0) - jnp.log1p(jnp.exp(-jnp.abs(x)))


def _nt_dot(a, b):
    return lax.dot_general(a, b, (((1,), (1,)), ((), ())), preferred_element_type=F32)


def _dot(a, b):
    return jnp.dot(a, b, preferred_element_type=F32)


def _mod_kernel(c_ref, w_ref, b_ref, o_ref):
    cv = c_ref[...]
    act = cv * _sigmoid(cv)
    o_ref[...] = jnp.dot(act, w_ref[...], precision=HIGHEST, preferred_element_type=F32) + b_ref[...]


def _modulation(cvecs, w_ada, b_ada):
    n_col = 512
    return pl.pallas_call(
        _mod_kernel,
        grid=(3 * D_MODEL // n_col,),
        in_specs=[
            pl.BlockSpec((8, D_MODEL), lambda j: (0, 0)),
            pl.BlockSpec((D_MODEL, n_col), lambda j: (0, j)),
            pl.BlockSpec((1, n_col), lambda j: (0, j)),
        ],
        out_specs=pl.BlockSpec((8, n_col), lambda j: (0, j)),
        out_shape=jax.ShapeDtypeStruct((8, 3 * D_MODEL), F32),
        name="modulation",
    )(cvecs, w_ada, b_ada)


def _segment_scan(x, op, reverse):
    n = x.shape[1]
    lane = lax.broadcasted_iota(jnp.int32, x.shape, 1) & (SCAN - 1)
    fill = 0.0 if op is jnp.add else MASKED
    k = 1
    while k < SCAN:
        if reverse:
            shifted = pltpu.roll(x, n - k, axis=1)
            valid = lane < SCAN - k
        else:
            shifted = pltpu.roll(x, k, axis=1)
            valid = lane >= k
        x = op(x, jnp.where(valid, shifted, fill))
        k *= 2
    return x


def _gate_scalars(g):
    lf = _log_sigmoid(g)
    pre = _segment_scan(lf, jnp.add, reverse=False)
    suf = _segment_scan(lf, jnp.add, reverse=True)
    total = pre + suf - lf
    fg = [slice(d * 8 + H_A, d * 8 + 2 * H_A) for d in range(2)]
    cum = jnp.concatenate([pre[fg[0]], suf[fg[1]]], axis=0)
    bend = jnp.concatenate([total[fg[0]], total[fg[1]]], axis=0)
    rr = jnp.concatenate([g[0:H_A], g[8:8 + H_A]], axis=0) - cum
    fmax = _segment_scan(rr, jnp.maximum, reverse=False)
    bmax = _segment_scan(rr, jnp.maximum, reverse=True)
    mloc = bend + jnp.maximum(fmax, bmax)
    pad = jnp.zeros((H_A, g.shape[1]), F32)
    out = []
    for d in range(2):
        rows = slice(d * H_A, (d + 1) * H_A)
        for q in (cum, rr, fmax if d == 0 else bmax, bend, mloc):
            out += [q[rows], pad]
    return jnp.concatenate(out, axis=0)


def _inproj_kernel(x_ref, mod_ref, ng_ref, wa_ref, wb_ref, wg_ref, bift_ref, gsgu_ref, ws_ref, bs_ref,
                   wpb_ref, qt_ref, k_ref, vt_ref, oz_ref, sga_ref, pb_ref, gs_ref, yb_scr, wqvt_scr, wgt_scr, *, tm):
    @pl.when(pl.program_id(0) == 0)
    def _():
        for i, j in enumerate((P_Q, P_V)):
            wqvt_scr[i] = wa_ref[:, j * D_MODEL:(j + 1) * D_MODEL].astype(F32).T.astype(BF16)
        wgt_scr[...] = wg_ref[...].astype(F32).T.astype(BF16)

    x = x_ref[...]
    xn = x * lax.rsqrt(jnp.mean(x * x, axis=-1, keepdims=True) + EPS) * ng_ref[...]
    shift = mod_ref[0, 0:1, :]
    scale = mod_ref[0, 1:2, :]
    hb = (xn * (1.0 + scale) + shift).astype(BF16)

    def proj(j):
        w_ref, col = (wa_ref, j) if j < N_PROJ // 2 else (wb_ref, j - N_PROJ // 2)
        return _dot(hb, w_ref[:, col * D_MODEL:(col + 1) * D_MODEL])

    qt_ref[...] = (_nt_dot(wqvt_scr[0], hb) * (DH_A ** -0.5)).astype(BF16)
    vt_ref[...] = _nt_dot(wqvt_scr[1], hb).astype(BF16)
    k_ref[...] = proj(P_K).astype(BF16)
    za = proj(P_ZA)
    oz_ref[...] = (_sigmoid(proj(P_O)) * (za * _sigmoid(za))).astype(BF16)
    gs_ref[...] = _gate_scalars(_nt_dot(wgt_scr[0:N_GATES, :], hb) + bift_ref[...])

    vb = proj(P_VB)
    vn = (vb * lax.rsqrt(jnp.mean(vb * vb, axis=-1, keepdims=True) + EPS) * gsgu_ref[...]).astype(BF16)
    u = proj(P_U)
    zb = proj(P_ZB)
    uz = u * (zb * _sigmoid(zb))
    for c in range(tm // CHUNK):
        rows = slice(c * CHUNK, (c + 1) * CHUNK)
        for g in range(G_B):
            cols = slice(g * DG_B, (g + 1) * DG_B)
            s = _dot(ws_ref[g], vn[rows, cols]) + bs_ref[:, cols]
            yb_scr[rows, cols] = (uz[rows, cols] * s).astype(BF16)
    pb = _sigmoid(proj(P_GB)) * _dot(yb_scr[...], wpb_ref[...])
    pb_ref[...] = pb.astype(BF16)
    sga_ref[...] = _sigmoid(proj(P_GA)).astype(BF16)


def _inproj(x2d, mod3, seq_tiles, ng, wa, wb, wg, bift, gsgu, ws, bs_full, wpb, *, tm):
    n_tok = x2d.shape[0]
    const2 = lambda i: (0, 0)
    const3 = lambda i: (0, 0, 0)
    resident = dict(pipeline_mode=pl.Buffered(1))
    tok_spec = pl.BlockSpec((tm, D_MODEL), lambda i: (i, 0))
    feat_spec = pl.BlockSpec((D_MODEL, tm), lambda i: (0, i))
    if mod3.shape[0] == 1:
        mod_map = const3
    else:
        mod_map = lambda i: (i // seq_tiles, 0, 0)
    tok_out = jax.ShapeDtypeStruct((n_tok, D_MODEL), BF16)
    feat_out = jax.ShapeDtypeStruct((D_MODEL, n_tok), BF16)
    return pl.pallas_call(
        functools.partial(_inproj_kernel, tm=tm),
        grid=(n_tok // tm,),
        in_specs=[
            tok_spec,
            pl.BlockSpec((1, 3, D_MODEL), mod_map),
            pl.BlockSpec((1, D_MODEL), const2),
            pl.BlockSpec((D_MODEL, N_PROJ // 2 * D_MODEL), const2, **resident),
            pl.BlockSpec((D_MODEL, N_PROJ // 2 * D_MODEL), const2, **resident),
            pl.BlockSpec((D_MODEL, GATE_LANES), const2),
            pl.BlockSpec((N_GATES, 1), const2),
            pl.BlockSpec((1, D_MODEL), const2),
            pl.BlockSpec((G_B, CHUNK, CHUNK), const3),
            pl.BlockSpec((CHUNK, D_MODEL), const2),
            pl.BlockSpec((D_MODEL, D_MODEL), const2, **resident),
        ],
        out_specs=[feat_spec, tok_spec, feat_spec, tok_spec, tok_spec, tok_spec,
                   pl.BlockSpec((GS_ROWS, tm), lambda i: (0, i))],
        out_shape=[feat_out, tok_out, feat_out, tok_out, tok_out, tok_out,
                   jax.ShapeDtypeStruct((GS_ROWS, n_tok), F32)],
        scratch_shapes=[pltpu.VMEM((tm, D_MODEL), BF16),
                        pltpu.VMEM((2, D_MODEL, D_MODEL), BF16),
                        pltpu.VMEM((GATE_LANES, D_MODEL), BF16)],
        compiler_params=pltpu.CompilerParams(
            dimension_semantics=("arbitrary",), vmem_limit_bytes=VMEM_LIMIT_BYTES),
        name="inproj",
    )(x2d, mod3, ng, wa, wb, wg, bift, gsgu, ws, bs_full, wpb)


def _store_state_bf16(kcn_scr, d, h, c_val, n_val):
    kcn_scr[d, h, SCAN:SCAN + DH_A, :] = c_val.astype(BF16)
    kcn_scr[d, h, SCAN + DH_A:SCAN + CN_ROWS, :] = jnp.broadcast_to(n_val, (BF16_ROWS, DH_A)).astype(BF16)


def _mlstm_sweep(d, tblk, qt_ref, k_ref, vt_ref, gs_ref, c_scr, n_scr, m_scr, kcn_scr, emit):
    n_seg = tblk // SCAN
    ri = lax.broadcasted_iota(jnp.int32, (SCAN, SCAN), 0)
    ci = lax.broadcasted_iota(jnp.int32, (SCAN, SCAN), 1)
    mask_st = (ri <= ci) if d == 0 else (ri >= ci)

    for i in range(n_seg):
        c = i if d == 0 else n_seg - 1 - i
        seg = slice(c * SCAN, (c + 1) * SCAN)

        def gate_scalar(q):
            g0 = d * GS_PER_DIR + q * 8
            return gs_ref[g0:g0 + H_A, seg]

        cum = gate_scalar(GS_CUM)
        rr = gate_scalar(GS_RR)
        b_end = gate_scalar(GS_BEND)
        m_prev = m_scr[d]
        a = cum + m_prev
        m_t = jnp.maximum(a, cum + gate_scalar(GS_CMAX))
        u = cum - m_t
        sc = jnp.exp(a - m_t)
        em = jnp.exp(-m_t)
        m_new = jnp.maximum(b_end + m_prev, gate_scalar(GS_MLOC))
        e_row = jnp.exp(b_end + rr - m_new)
        sp = jnp.exp(b_end + m_prev - m_new)[:, 0:1]
        m_scr[d] = m_new

        for h in range(H_A):
            hs = slice(h * DH_A, (h + 1) * DH_A)
            row = slice(h, h + 1)
            qt = qt_ref[hs, seg]
            kh = k_ref[seg, hs]
            vt = vt_ref[hs, seg]

            rr_st = jnp.broadcast_to(rr[row], (SCAN, SCAN)).T
            decay = jnp.exp(jnp.where(mask_st, u[row] + rr_st, MASKED))
            kcn_scr[d, h, 0:SCAN, :] = kh
            kcq = _dot(kcn_scr[d, h], qt)
            s_t = kcq[0:SCAN] * decay
            cq = kcq[SCAN:SCAN + DH_A]
            den = jnp.sum(s_t, axis=0, keepdims=True) + sc[row] * kcq[SCAN + DH_A:SCAN + DH_A + 1]
            inv = 1.0 / jnp.maximum(jnp.abs(den), em[row])
            num = _dot(vt, s_t.astype(BF16)) + sc[row] * cq
            emit(c, h, num * inv)

            vte = (vt.astype(F32) * e_row[row]).astype(BF16)
            lhs = jnp.concatenate([vte, jnp.broadcast_to(e_row[row], (BF16_ROWS, SCAN)).astype(BF16)], axis=0)
            upd = _dot(lhs, kh)
            c_new = sp[row] * c_scr[d, h] + upd[0:DH_A]
            n_new = sp[row] * n_scr[d, row, :] + upd[DH_A:DH_A + 1]
            c_scr[d, h] = c_new
            n_scr[d, row, :] = n_new
            _store_state_bf16(kcn_scr, d, h, c_new, n_new)


def _mixer_kernel(*refs, nblk, tblk, has_init, emit_state):
    refs = list(refs)
    qt_ref, k_ref, vt_ref, oz_ref, sga_ref, pb_ref, gs_ref, x_ref, mod_ref = refs[:9]
    refs = refs[9:]
    if has_init:
        c0_ref, n0_ref, m0_ref = refs[:3]
        refs = refs[3:]
    wpa_ref, wout_ref, nf_ref, y_ref = refs[:4]
    refs = refs[4:]
    if emit_state:
        cf_ref, nfin_ref, mfin_ref = refs[:3]
        refs = refs[3:]
    c_scr, n_scr, m_scr, kcn_scr, hb_scr, ya_scr = refs

    n_seg = tblk // SCAN
    sweep = functools.partial(_mlstm_sweep, tblk=tblk, qt_ref=qt_ref, k_ref=k_ref, vt_ref=vt_ref, gs_ref=gs_ref,
                              c_scr=c_scr, n_scr=n_scr, m_scr=m_scr, kcn_scr=kcn_scr)

    def init_state():
        if has_init:
            c_scr[...] = c0_ref[0]
            n_scr[...] = n0_ref[0]
            m_scr[...] = m0_ref[0]
        else:
            c_scr[...] = jnp.zeros_like(c_scr)
            n_scr[...] = jnp.zeros_like(n_scr)
            m_scr[...] = jnp.zeros_like(m_scr)
        for d in range(2):
            for h in range(H_A):
                _store_state_bf16(kcn_scr, d, h, c_scr[d, h], n_scr[d, h:h + 1, :])

    def backward(blk):
        def emit(c, h, h_t):
            hb_scr[blk * n_seg + c, h * DH_A:(h + 1) * DH_A, :] = h_t.astype(BF16)

        sweep(1, emit=emit)

    def forward(blk):
        def emit(c, h, h_t):
            hs = slice(h * DH_A, (h + 1) * DH_A)
            rows = slice(c * SCAN, (c + 1) * SCAN)
            h_a = (h_t + hb_scr[blk * n_seg + c, hs, :].astype(F32)).T
            ya_scr[rows, hs] = (oz_ref[rows, hs].astype(F32) * h_a).astype(BF16)

        sweep(0, emit=emit)

        pa = _dot(ya_scr[...], wpa_ref[...])
        merged = sga_ref[...].astype(F32) * pa + pb_ref[...].astype(F32)
        upd = _dot(merged.astype(BF16), wout_ref[...])
        out = x_ref[...] + mod_ref[0, 2:3, :] * upd
        y_ref[...] = out * lax.rsqrt(jnp.mean(out * out, axis=-1, keepdims=True) + EPS) * nf_ref[...]

    def write_state():
        cf_ref[0] = c_scr[...]
        nfin_ref[0] = n_scr[...]
        mfin_ref[0] = m_scr[...]

    if nblk == 1:
        init_state()
        backward(0)
        forward(0)
        if emit_state:
            write_state()
        return

    step = pl.program_id(1)
    pl.when(step == 0)(init_state)
    pl.when(step < nblk)(lambda: backward(nblk - 1 - step))
    pl.when(step >= nblk)(lambda: forward(step - nblk))
    if emit_state:
        pl.when(step == 2 * nblk - 1)(write_state)


def _mixer(qt, k, vt, oz, sga, pb, gs, x2d, mod3, init_state, wpa, wout, norm_f, *, n_seq, seq_len, tblk,
           emit_state):
    nblk = seq_len // tblk
    has_init = init_state is not None
    n_steps = 1 if nblk == 1 else 2 * nblk

    def sweep_blk(b, s):
        if nblk == 1:
            return b
        return b * nblk + jnp.where(s < nblk, nblk - 1 - s, s - nblk)

    def tail_blk(b, s):
        if nblk == 1:
            return b
        return b * nblk + jnp.maximum(s - nblk, 0)

    sweep_tok = pl.BlockSpec((tblk, D_MODEL), lambda b, s: (sweep_blk(b, s), 0))
    sweep_feat = pl.BlockSpec((D_MODEL, tblk), lambda b, s: (0, sweep_blk(b, s)))
    tail_spec = pl.BlockSpec((tblk, D_MODEL), lambda b, s: (tail_blk(b, s), 0))
    const2 = lambda b, s: (0, 0)
    if mod3.shape[0] == 1:
        mod_map = lambda b, s: (0, 0, 0)
    else:
        mod_map = lambda b, s: (b, 0, 0)
    c_spec = pl.BlockSpec((1, 2, H_A, DH_A, DH_A), lambda b, s: (b, 0, 0, 0, 0))
    n_spec = pl.BlockSpec((1, 2, H_A, DH_A), lambda b, s: (b, 0, 0, 0))
    m_spec = pl.BlockSpec((1, 2, H_A, SCAN), lambda b, s: (b, 0, 0, 0))

    in_specs = [
        sweep_feat, sweep_tok, sweep_feat,
        tail_spec, tail_spec, tail_spec,
        pl.BlockSpec((GS_ROWS, tblk), lambda b, s: (0, sweep_blk(b, s))),
        tail_spec,
        pl.BlockSpec((1, 3, D_MODEL), mod_map),
    ]
    args = [qt, k, vt, oz, sga, pb, gs, x2d, mod3]
    if has_init:
        in_specs += [c_spec, n_spec, m_spec]
        args += list(init_state)
    in_specs += [
        pl.BlockSpec((D_MODEL, D_MODEL), const2),
        pl.BlockSpec((D_MODEL, D_MODEL), const2),
        pl.BlockSpec((1, D_MODEL), const2),
    ]
    args += [wpa, wout, norm_f]

    out_specs = [tail_spec]
    out_shape = [jax.ShapeDtypeStruct((n_seq * seq_len, D_MODEL), F32)]
    if emit_state:
        out_specs += [c_spec, n_spec, m_spec]
        out_shape += [
            jax.ShapeDtypeStruct((n_seq, 2, H_A, DH_A, DH_A), F32),
            jax.ShapeDtypeStruct((n_seq, 2, H_A, DH_A), F32),
            jax.ShapeDtypeStruct((n_seq, 2, H_A, SCAN), F32),
        ]

    return pl.pallas_call(
        functools.partial(_mixer_kernel, nblk=nblk, tblk=tblk, has_init=has_init, emit_state=emit_state),
        grid=(n_seq, n_steps),
        in_specs=in_specs,
        out_specs=out_specs,
        out_shape=out_shape,
        scratch_shapes=[
            pltpu.VMEM((2, H_A, DH_A, DH_A), F32),
            pltpu.VMEM((2, H_A, DH_A), F32),
            pltpu.VMEM((2, H_A, SCAN), F32),
            pltpu.VMEM((2, H_A, SCAN + CN_ROWS, DH_A), BF16),
            pltpu.VMEM((seq_len // SCAN, D_MODEL, SCAN), BF16),
            pltpu.VMEM((tblk, D_MODEL), BF16),
        ],
        compiler_params=pltpu.CompilerParams(
            dimension_semantics=("arbitrary", "arbitrary"), vmem_limit_bytes=VMEM_LIMIT_BYTES),
        name="mixer",
    )(*args)


def _split_w_in(w):
    gate0 = N_PROJ // 2 * D_MODEL
    rest0 = gate0 + N_GATES
    wa = w[:, :gate0].astype(BF16)
    wb = w[:, rest0:].astype(BF16)
    wg = jnp.pad(w[:, gate0:rest0], ((0, 0), (0, GATE_LANES - N_GATES))).astype(BF16)
    return wa, wb, wg


def kernel(x_prompt, x_sample, c, state_C, state_n, state_m, c_ctx, norm_g, w_ada, b_ada, w_in, b_if, g_sgu, w_s, b_s,
           w_proj_a, w_proj_b, w_out, norm_f):
    n_ctx, t_ctx, _ = x_prompt.shape
    n_lat, t_lat, _ = x_sample.shape
    depth = norm_g.shape[0]
    assert depth == 1, "single-layer step"
    l = 0

    cvecs = jnp.concatenate([c, c_ctx[None], jnp.zeros((8 - n_lat - 1, D_MODEL), F32)], axis=0)
    mod = _modulation(cvecs, w_ada[l], b_ada[l][None]).reshape(8, 3, D_MODEL)
    mod_lat = mod[:n_lat]
    mod_ctx = mod[n_lat:n_lat + 1]

    wa, wb, wg = _split_w_in(w_in[l])
    bift = b_if[l].reshape(N_GATES, 1)
    ng = norm_g[l][None]
    gsgu = g_sgu[l][None]
    ws = w_s[l].astype(BF16)
    bs_full = jnp.repeat(jnp.transpose(b_s[l]), DG_B, axis=1)
    wpa = w_proj_a[l].astype(BF16)
    wpb = w_proj_b[l].astype(BF16)
    wout = w_out[l].astype(BF16)
    nf = norm_f[None]

    tm = 512
    xp2 = x_prompt.reshape(n_ctx * t_ctx, D_MODEL)
    xs2 = x_sample.reshape(n_lat * t_lat, D_MODEL)
    shared = (ng, wa, wb, wg, bift, gsgu, ws, bs_full, wpb)
    pc = _inproj(xp2, mod_ctx, 1, *shared, tm=tm)
    ps = _inproj(xs2, mod_lat, t_lat // tm, *shared, tm=tm)

    y_p, c_new, n_new, m_new = _mixer(*pc, xp2, mod_ctx, None, wpa, wout, nf, n_seq=n_ctx, seq_len=t_ctx,
                                      tblk=t_ctx, emit_state=True)
    init = (state_C[:, l], state_n[:, l],
            jnp.broadcast_to(state_m[:, l][..., None], (n_lat, 2, H_A, SCAN)))
    (y_s,) = _mixer(*ps, xs2, mod_lat, init, wpa, wout, nf, n_seq=n_lat, seq_len=t_lat, tblk=512,
                    emit_state=False)

    return (y_p.reshape(n_ctx, t_ctx, D_MODEL), y_s.reshape(n_lat, t_lat, D_MODEL),
            c_new[:, None], n_new[:, None], m_new[:, None, :, :, 0])
```

```python
import functools

import jax
import jax.numpy as jnp
from jax import lax
from jax.experimental import pallas as pl
from jax.experimental.pallas import tpu as pltpu

D_MODEL = 1024
H_A = 4
DH_A = 256
G_B = 4
DG_B = 256
CHUNK = 128
SCAN = 256
N_GATES = 16
GATE_LANES = 128
EPS = 1e-6
MASKED = -1e30
F32 = jnp.float32
BF16 = jnp.bfloat16
HIGHEST = lax.Precision.HIGHEST
VMEM_LIMIT_BYTES = 60000 * 1024
BF16_ROWS = 16
CN_ROWS = DH_A + BF16_ROWS
GS_PER_DIR = 40
GS_ROWS = 2 * GS_PER_DIR
GS_CUM, GS_RR, GS_CMAX, GS_BEND, GS_MLOC = range(5)

N_PROJ = 10
P_Q, P_K, P_V, P_O, P_ZA, P_U, P_VB, P_ZB, P_GA, P_GB = range(N_PROJ)


def _sigmoid(x):
    return 0.5 * jnp.tanh(0.5 * x) + 0.5


def _log_sigmoid(x):
    return jnp.minimum(x, 0.0) - jnp.log1p(jnp.exp(-jnp.abs(x)))


def _nt_dot(a, b):
    return lax.dot_general(a, b, (((1,), (1,)), ((), ())), preferred_element_type=F32)


def _dot(a, b):
    return jnp.dot(a, b, preferred_element_type=F32)


def _mod_kernel(c_ref, w_ref, b_ref, o_ref):
    cv = c_ref[...]
    act = cv * _sigmoid(cv)
    o_ref[...] = jnp.dot(act, w_ref[...], precision=HIGHEST, preferred_element_type=F32) + b_ref[...]


def _modulation(cvecs, w_ada, b_ada):
    n_col = 512
    return pl.pallas_call(
        _mod_kernel,
        grid=(3 * D_MODEL // n_col,),
        in_specs=[
            pl.BlockSpec((8, D_MODEL), lambda j: (0, 0)),
            pl.BlockSpec((D_MODEL, n_col), lambda j: (0, j)),
            pl.BlockSpec((1, n_col), lambda j: (0, j)),
        ],
        out_specs=pl.BlockSpec((8, n_col), lambda j: (0, j)),
        out_shape=jax.ShapeDtypeStruct((8, 3 * D_MODEL), F32),
        name="modulation",
    )(cvecs, w_ada, b_ada)


def _segment_scan(x, op, reverse):
    n = x.shape[1]
    lane = lax.broadcasted_iota(jnp.int32, x.shape, 1) & (SCAN - 1)
    fill = 0.0 if op is jnp.add else MASKED
    k = 1
    while k < SCAN:
        if reverse:
            shifted = pltpu.roll(x, n - k, axis=1)
            valid = lane < SCAN - k
        else:
            shifted = pltpu.roll(x, k, axis=1)
            valid = lane >= k
        x = op(x, jnp.where(valid, shifted, fill))
        k *= 2
    return x


def _gate_scalars(g):
    lf = _log_sigmoid(g)
    pre = _segment_scan(lf, jnp.add, reverse=False)
    suf = _segment_scan(lf, jnp.add, reverse=True)
    total = pre + suf - lf
    fg = [slice(d * 8 + H_A, d * 8 + 2 * H_A) for d in range(2)]
    cum = jnp.concatenate([pre[fg[0]], suf[fg[1]]], axis=0)
    bend = jnp.concatenate([total[fg[0]], total[fg[1]]], axis=0)
    rr = jnp.concatenate([g[0:H_A], g[8:8 + H_A]], axis=0) - cum
    fmax = _segment_scan(rr, jnp.maximum, reverse=False)
    bmax = _segment_scan(rr, jnp.maximum, reverse=True)
    mloc = bend + jnp.maximum(fmax, bmax)
    pad = jnp.zeros((H_A, g.shape[1]), F32)
    out = []
    for d in range(2):
        rows = slice(d * H_A, (d + 1) * H_A)
        for q in (cum, rr, fmax if d == 0 else bmax, bend, mloc):
            out += [q[rows], pad]
    return jnp.concatenate(out, axis=0)


def _inproj_kernel(x0_ref, mod0_ref, xn_ref, modn_ref, ng_ref, wa_ref, wb_ref, wg_ref, bift_ref, gsgu_ref, ws_ref,
                   bs_ref, wpb_ref, qt_ref, k_ref, vt_ref, oz_ref, sga_ref, pb_ref, gs_ref,
                   yb_scr, wqvt_scr, wgt_scr, hb_scr, *, tm):
    step = pl.program_id(0)

    def modulated_norm(x_ref, mod_ref):
        x = x_ref[...]
        xn = x * lax.rsqrt(jnp.mean(x * x, axis=-1, keepdims=True) + EPS) * ng_ref[...]
        return (xn * (1.0 + mod_ref[0, 1:2, :]) + mod_ref[0, 0:1, :]).astype(BF16)

    @pl.when(step == 0)
    def _():
        for i, j in enumerate((P_Q, P_V)):
            wqvt_scr[i] = wa_ref[:, j * D_MODEL:(j + 1) * D_MODEL].astype(F32).T.astype(BF16)
        wgt_scr[...] = wg_ref[...].astype(F32).T.astype(BF16)
        hb_scr[...] = modulated_norm(x0_ref, mod0_ref)

    hb_next = modulated_norm(xn_ref, modn_ref)

    def proj(j):
        w_ref, col = (wa_ref, j) if j < N_PROJ // 2 else (wb_ref, j - N_PROJ // 2)
        return _dot(hb_scr[...], w_ref[:, col * D_MODEL:(col + 1) * D_MODEL])

    za = proj(P_ZA)
    oz_ref[...] = (_sigmoid(proj(P_O)) * (za * _sigmoid(za))).astype(BF16)
    gs_ref[...] = _gate_scalars(_nt_dot(wgt_scr[0:N_GATES, :], hb_scr[...]) + bift_ref[...])

    vb = proj(P_VB)
    vn = (vb * lax.rsqrt(jnp.mean(vb * vb, axis=-1, keepdims=True) + EPS) * gsgu_ref[...]).astype(BF16)
    u = proj(P_U)
    zb = proj(P_ZB)
    uz = u * (zb * _sigmoid(zb))
    for c in range(tm // CHUNK):
        rows = slice(c * CHUNK, (c + 1) * CHUNK)
        for g in range(G_B):
            cols = slice(g * DG_B, (g + 1) * DG_B)
            s = _dot(ws_ref[g], vn[rows, cols]) + bs_ref[:, cols]
            yb_scr[rows, cols] = (uz[rows, cols] * s).astype(BF16)
    gate_b = _sigmoid(proj(P_GB))
    sga_ref[...] = _sigmoid(proj(P_GA)).astype(BF16)
    qt_ref[...] = (_nt_dot(wqvt_scr[0], hb_scr[...]) * (DH_A ** -0.5)).astype(BF16)
    vt_ref[...] = _nt_dot(wqvt_scr[1], hb_scr[...]).astype(BF16)
    k_ref[...] = proj(P_K).astype(BF16)
    hb_scr[...] = hb_next
    pb_ref[...] = (gate_b * _dot(yb_scr[...], wpb_ref[...])).astype(BF16)


def _inproj(x2d, mod3, seq_tiles, ng, wa, wb, wg, bift, gsgu, ws, bs_full, wpb, *, tm):
    n_tok = x2d.shape[0]
    n_tiles = n_tok // tm
    const2 = lambda i: (0, 0)
    const3 = lambda i: (0, 0, 0)
    resident = dict(pipeline_mode=pl.Buffered(1))
    tok_spec = pl.BlockSpec((tm, D_MODEL), lambda i: (i, 0))
    feat_spec = pl.BlockSpec((D_MODEL, tm), lambda i: (0, i))
    next_tile = lambda i: jnp.minimum(i + 1, n_tiles - 1)
    if mod3.shape[0] == 1:
        next_mod_map = const3
    else:
        next_mod_map = lambda i: (next_tile(i) // seq_tiles, 0, 0)
    tok_out = jax.ShapeDtypeStruct((n_tok, D_MODEL), BF16)
    feat_out = jax.ShapeDtypeStruct((D_MODEL, n_tok), BF16)
    return pl.pallas_call(
        functools.partial(_inproj_kernel, tm=tm),
        grid=(n_tiles,),
        in_specs=[
            pl.BlockSpec((tm, D_MODEL), const2),
            pl.BlockSpec((1, 3, D_MODEL), const3),
            pl.BlockSpec((tm, D_MODEL), lambda i: (next_tile(i), 0)),
            pl.BlockSpec((1, 3, D_MODEL), next_mod_map),
            pl.BlockSpec((1, D_MODEL), const2),
            pl.BlockSpec((D_MODEL, N_PROJ // 2 * D_MODEL), const2, **resident),
            pl.BlockSpec((D_MODEL, N_PROJ // 2 * D_MODEL), const2, **resident),
            pl.BlockSpec((D_MODEL, GATE_LANES), const2),
            pl.BlockSpec((N_GATES, 1), const2),
            pl.BlockSpec((1, D_MODEL), const2),
            pl.BlockSpec((G_B, CHUNK, CHUNK), const3),
            pl.BlockSpec((CHUNK, D_MODEL), const2),
            pl.BlockSpec((D_MODEL, D_MODEL), const2, **resident),
        ],
        out_specs=[feat_spec, tok_spec, feat_spec, tok_spec, tok_spec, tok_spec,
                   pl.BlockSpec((GS_ROWS, tm), lambda i: (0, i))],
        out_shape=[feat_out, tok_out, feat_out, tok_out, tok_out, tok_out,
                   jax.ShapeDtypeStruct((GS_ROWS, n_tok), F32)],
        scratch_shapes=[pltpu.VMEM((tm, D_MODEL), BF16),
                        pltpu.VMEM((2, D_MODEL, D_MODEL), BF16),
                        pltpu.VMEM((GATE_LANES, D_MODEL), BF16),
                        pltpu.VMEM((tm, D_MODEL), BF16)],
        compiler_params=pltpu.CompilerParams(
            dimension_semantics=("arbitrary",), vmem_limit_bytes=VMEM_LIMIT_BYTES),
        name="inproj",
    )(x2d, mod3, x2d, mod3, ng, wa, wb, wg, bift, gsgu, ws, bs_full, wpb)


def _store_state_bf16(kcn_scr, d, h, c_val, n_val):
    kcn_scr[d, h, SCAN:SCAN + DH_A, :] = c_val.astype(BF16)
    kcn_scr[d, h, SCAN + DH_A:SCAN + CN_ROWS, :] = jnp.broadcast_to(n_val, (BF16_ROWS, DH_A)).astype(BF16)


def _mlstm_sweep(d, tblk, qt_ref, k_ref, vt_ref, gs_ref, c_scr, n_scr, m_scr, kcn_scr, emit):
    n_seg = tblk // SCAN
    ri = lax.broadcasted_iota(jnp.int32, (SCAN, SCAN), 0)
    ci = lax.broadcasted_iota(jnp.int32, (SCAN, SCAN), 1)
    mask_st = (ri <= ci) if d == 0 else (ri >= ci)

    for i in range(n_seg):
        c = i if d == 0 else n_seg - 1 - i
        seg = slice(c * SCAN, (c + 1) * SCAN)

        def gate_scalar(q):
            g0 = d * GS_PER_DIR + q * 8
            return gs_ref[g0:g0 + H_A, seg]

        cum = gate_scalar(GS_CUM)
        rr = gate_scalar(GS_RR)
        b_end = gate_scalar(GS_BEND)
        m_prev = m_scr[d]
        a = cum + m_prev
        m_t = jnp.maximum(a, cum + gate_scalar(GS_CMAX))
        u = cum - m_t
        sc = jnp.exp(a - m_t)
        em = jnp.exp(-m_t)
        m_new = jnp.maximum(b_end + m_prev, gate_scalar(GS_MLOC))
        e_row = jnp.exp(b_end + rr - m_new)
        sp = jnp.exp(b_end + m_prev - m_new)[:, 0:1]
        m_scr[d] = m_new

        for h in range(H_A):
            hs = slice(h * DH_A, (h + 1) * DH_A)
            row = slice(h, h + 1)
            qt = qt_ref[hs, seg]
            kh = k_ref[seg, hs]
            vt = vt_ref[hs, seg]

            rr_st = jnp.broadcast_to(rr[row], (SCAN, SCAN)).T
            decay = jnp.exp(jnp.where(mask_st, u[row] + rr_st, MASKED))
            kcn_scr[d, h, 0:SCAN, :] = kh
            kcq = _dot(kcn_scr[d, h], qt)
            s_t = kcq[0:SCAN] * decay
            cq = kcq[SCAN:SCAN + DH_A]
            den = jnp.sum(s_t, axis=0, keepdims=True) + sc[row] * kcq[SCAN + DH_A:SCAN + DH_A + 1]
            inv = 1.0 / jnp.maximum(jnp.abs(den), em[row])
            num = _dot(vt, s_t.astype(BF16)) + sc[row] * cq
            emit(c, h, num * inv)

            vte = (vt.astype(F32) * e_row[row]).astype(BF16)
            lhs = jnp.concatenate([vte, jnp.broadcast_to(e_row[row], (BF16_ROWS, SCAN)).astype(BF16)], axis=0)
            upd = _dot(lhs, kh)
            c_new = sp[row] * c_scr[d, h] + upd[0:DH_A]
            n_new = sp[row] * n_scr[d, row, :] + upd[DH_A:DH_A + 1]
            c_scr[d, h] = c_new
            n_scr[d, row, :] = n_new
            _store_state_bf16(kcn_scr, d, h, c_new, n_new)


def _mixer_kernel(*refs, nblk, tblk, has_init, emit_state):
    refs = list(refs)
    qt_ref, k_ref, vt_ref, oz_ref, sga_ref, pb_ref, gs_ref, x_ref, mod_ref = refs[:9]
    refs = refs[9:]
    if has_init:
        c0_ref, n0_ref, m0_ref = refs[:3]
        refs = refs[3:]
    wpa_ref, wout_ref, nf_ref, y_ref = refs[:4]
    refs = refs[4:]
    if emit_state:
        cf_ref, nfin_ref, mfin_ref = refs[:3]
        refs = refs[3:]
    c_scr, n_scr, m_scr, kcn_scr, hb_scr, ya_scr = refs

    n_seg = tblk // SCAN
    sweep = functools.partial(_mlstm_sweep, tblk=tblk, qt_ref=qt_ref, k_ref=k_ref, vt_ref=vt_ref, gs_ref=gs_ref,
                              c_scr=c_scr, n_scr=n_scr, m_scr=m_scr, kcn_scr=kcn_scr)

    def init_state():
        if has_init:
            c_scr[...] = c0_ref[0]
            n_scr[...] = n0_ref[0]
            m_scr[...] = m0_ref[0]
        else:
            c_scr[...] = jnp.zeros_like(c_scr)
            n_scr[...] = jnp.zeros_like(n_scr)
            m_scr[...] = jnp.zeros_like(m_scr)
        for d in range(2):
            for h in range(H_A):
                _store_state_bf16(kcn_scr, d, h, c_scr[d, h], n_scr[d, h:h + 1, :])

    def backward(blk):
        def emit(c, h, h_t):
            hb_scr[blk * n_seg + c, h * DH_A:(h + 1) * DH_A, :] = h_t.astype(BF16)

        sweep(1, emit=emit)

    def forward(blk):
        def emit(c, h, h_t):
            hs = slice(h * DH_A, (h + 1) * DH_A)
            rows = slice(c * SCAN, (c + 1) * SCAN)
            h_a = (h_t + hb_scr[blk * n_seg + c, hs, :].astype(F32)).T
            ya_scr[rows, hs] = (oz_ref[rows, hs].astype(F32) * h_a).astype(BF16)

        sweep(0, emit=emit)

        pa = _dot(ya_scr[...], wpa_ref[...])
        merged = sga_ref[...].astype(F32) * pa + pb_ref[...].astype(F32)
        upd = _dot(merged.astype(BF16), wout_ref[...])
        out = x_ref[...] + mod_ref[0, 2:3, :] * upd
        y_ref[...] = out * lax.rsqrt(jnp.mean(out * out, axis=-1, keepdims=True) + EPS) * nf_ref[...]

    def write_state():
        cf_ref[0] = c_scr[...]
        nfin_ref[0] = n_scr[...]
        mfin_ref[0] = m_scr[...]

    if nblk == 1:
        init_state()
        backward(0)
        forward(0)
        if emit_state:
            write_state()
        return

    step = pl.program_id(1)
    pl.when(step == 0)(init_state)
    pl.when(step < nblk)(lambda: backward(nblk - 1 - step))
    pl.when(step >= nblk)(lambda: forward(step - nblk))
    if emit_state:
        pl.when(step == 2 * nblk - 1)(write_state)


def _mixer(qt, k, vt, oz, sga, pb, gs, x2d, mod3, init_state, wpa, wout, norm_f, *, n_seq, seq_len, tblk,
           emit_state):
    nblk = seq_len // tblk
    has_init = init_state is not None
    n_steps = 1 if nblk == 1 else 2 * nblk

    def sweep_blk(b, s):
        if nblk == 1:
            return b
        return b * nblk + jnp.where(s < nblk, nblk - 1 - s, s - nblk)

    def tail_blk(b, s):
        if nblk == 1:
            return b
        return b * nblk + jnp.maximum(s - nblk, 0)

    sweep_tok = pl.BlockSpec((tblk, D_MODEL), lambda b, s: (sweep_blk(b, s), 0))
    sweep_feat = pl.BlockSpec((D_MODEL, tblk), lambda b, s: (0, sweep_blk(b, s)))
    tail_spec = pl.BlockSpec((tblk, D_MODEL), lambda b, s: (tail_blk(b, s), 0))
    const2 = lambda b, s: (0, 0)
    if mod3.shape[0] == 1:
        mod_map = lambda b, s: (0, 0, 0)
    else:
        mod_map = lambda b, s: (b, 0, 0)
    c_spec = pl.BlockSpec((1, 2, H_A, DH_A, DH_A), lambda b, s: (b, 0, 0, 0, 0))
    n_spec = pl.BlockSpec((1, 2, H_A, DH_A), lambda b, s: (b, 0, 0, 0))
    m_spec = pl.BlockSpec((1, 2, H_A, SCAN), lambda b, s: (b, 0, 0, 0))

    in_specs = [
        sweep_feat, sweep_tok, sweep_feat,
        tail_spec, tail_spec, tail_spec,
        pl.BlockSpec((GS_ROWS, tblk), lambda b, s: (0, sweep_blk(b, s))),
        tail_spec,
        pl.BlockSpec((1, 3, D_MODEL), mod_map),
    ]
    args = [qt, k, vt, oz, sga, pb, gs, x2d, mod3]
    if has_init:
        in_specs += [c_spec, n_spec, m_spec]
        args += list(init_state)
    in_specs += [
        pl.BlockSpec((D_MODEL, D_MODEL), const2),
        pl.BlockSpec((D_MODEL, D_MODEL), const2),
        pl.BlockSpec((1, D_MODEL), const2),
    ]
    args += [wpa, wout, norm_f]

    out_specs = [tail_spec]
    out_shape = [jax.ShapeDtypeStruct((n_seq * seq_len, D_MODEL), F32)]
    if emit_state:
        out_specs += [c_spec, n_spec, m_spec]
        out_shape += [
            jax.ShapeDtypeStruct((n_seq, 2, H_A, DH_A, DH_A), F32),
            jax.ShapeDtypeStruct((n_seq, 2, H_A, DH_A), F32),
            jax.ShapeDtypeStruct((n_seq, 2, H_A, SCAN), F32),
        ]

    return pl.pallas_call(
        functools.partial(_mixer_kernel, nblk=nblk, tblk=tblk, has_init=has_init, emit_state=emit_state),
        grid=(n_seq, n_steps),
        in_specs=in_specs,
        out_specs=out_specs,
        out_shape=out_shape,
        scratch_shapes=[
            pltpu.VMEM((2, H_A, DH_A, DH_A), F32),
            pltpu.VMEM((2, H_A, DH_A), F32),
            pltpu.VMEM((2, H_A, SCAN), F32),
            pltpu.VMEM((2, H_A, SCAN + CN_ROWS, DH_A), BF16),
            pltpu.VMEM((seq_len // SCAN, D_MODEL, SCAN), BF16),
            pltpu.VMEM((tblk, D_MODEL), BF16),
        ],
        compiler_params=pltpu.CompilerParams(
            dimension_semantics=("arbitrary", "arbitrary"), vmem_limit_bytes=VMEM_LIMIT_BYTES),
        name="mixer",
    )(*args)


def _split_w_in_kernel(w_ref, wa_ref, wb_ref, wg_ref):
    gate0 = N_PROJ // 2 * D_MODEL
    rest0 = gate0 + N_GATES
    wa_ref[...] = w_ref[:, 0:gate0].astype(BF16)
    wb_ref[...] = w_ref[:, rest0:rest0 + gate0].astype(BF16)
    g = w_ref[:, gate0:gate0 + GATE_LANES]
    lane = lax.broadcasted_iota(jnp.int32, g.shape, 1)
    wg_ref[...] = jnp.where(lane < N_GATES, g, 0.0).astype(BF16)


def _split_w_in(w):
    rows = 128
    half = N_PROJ // 2 * D_MODEL
    return pl.pallas_call(
        _split_w_in_kernel,
        grid=(D_MODEL // rows,),
        in_specs=[pl.BlockSpec((rows, w.shape[1]), lambda i: (i, 0))],
        out_specs=[pl.BlockSpec((rows, half), lambda i: (i, 0)),
                   pl.BlockSpec((rows, half), lambda i: (i, 0)),
                   pl.BlockSpec((rows, GATE_LANES), lambda i: (i, 0))],
        out_shape=[jax.ShapeDtypeStruct((D_MODEL, half), BF16),
                   jax.ShapeDtypeStruct((D_MODEL, half), BF16),
                   jax.ShapeDtypeStruct((D_MODEL, GATE_LANES), BF16)],
        name="split_w_in",
    )(w)


def kernel(x_prompt, x_sample, c, state_C, state_n, state_m, c_ctx, norm_g, w_ada, b_ada, w_in, b_if, g_sgu, w_s, b_s,
           w_proj_a, w_proj_b, w_out, norm_f):
    n_ctx, t_ctx, _ = x_prompt.shape
    n_lat, t_lat, _ = x_sample.shape
    depth = norm_g.shape[0]
    assert depth == 1, "single-layer step"
    l = 0

    cvecs = jnp.concatenate([c, c_ctx[None], jnp.zeros((8 - n_lat - 1, D_MODEL), F32)], axis=0)
    mod = _modulation(cvecs, w_ada[l], b_ada[l][None]).reshape(8, 3, D_MODEL)
    mod_lat = mod[:n_lat]
    mod_ctx = mod[n_lat:n_lat + 1]

    wa, wb, wg = _split_w_in(w_in[l])
    bift = b_if[l].reshape(N_GATES, 1)
    ng = norm_g[l][None]
    gsgu = g_sgu[l][None]
    ws = w_s[l].astype(BF16)
    bs_full = jnp.repeat(jnp.transpose(b_s[l]), DG_B, axis=1)
    wpa = w_proj_a[l].astype(BF16)
    wpb = w_proj_b[l].astype(BF16)
    wout = w_out[l].astype(BF16)
    nf = norm_f[None]

    tm = 512
    xp2 = x_prompt.reshape(n_ctx * t_ctx, D_MODEL)
    xs2 = x_sample.reshape(n_lat * t_lat, D_MODEL)
    shared = (ng, wa, wb, wg, bift, gsgu, ws, bs_full, wpb)
    pc = _inproj(xp2, mod_ctx, 1, *shared, tm=tm)
    ps = _inproj(xs2, mod_lat, t_lat // tm, *shared, tm=tm)

    y_p, c_new, n_new, m_new = _mixer(*pc, xp2, mod_ctx, None, wpa, wout, nf, n_seq=n_ctx, seq_len=t_ctx,
                                      tblk=t_ctx, emit_state=True)
    init = (state_C[:, l], state_n[:, l],
            jnp.broadcast_to(state_m[:, l][..., None], (n_lat, 2, H_A, SCAN)))
    (y_s,) = _mixer(*ps, xs2, mod_lat, init, wpa, wout, nf, n_seq=n_lat, seq_len=t_lat, tblk=512,
                    emit_state=False)

    return (y_p.reshape(n_ctx, t_ctx, D_MODEL), y_s.reshape(n_lat, t_lat, D_MODEL),
            c_new[:, None], n_new[:, None], m_new[:, None, :, :, 0])
```

```python
import functools

import jax
import jax.numpy as jnp
from jax import lax
from jax.experimental import pallas as pl
from jax.experimental.pallas import tpu as pltpu

D_MODEL = 1024
H_A = 4
DH_A = 256
G_B = 4
DG_B = 256
CHUNK = 128
SCAN = 256
N_GATES = 16
GATE_LANES = 128
EPS = 1e-6
MASKED = -1e30
F32 = jnp.float32
BF16 = jnp.bfloat16
HIGHEST = lax.Precision.HIGHEST
VMEM_LIMIT_BYTES = 60000 * 1024
BF16_ROWS = 16
CN_ROWS = DH_A + BF16_ROWS
GS_PER_DIR = 40
GS_ROWS = 2 * GS_PER_DIR
GS_CUM, GS_RR, GS_CMAX, GS_BEND, GS_MLOC = range(5)

N_PROJ = 10
P_Q, P_K, P_V, P_O, P_ZA, P_U, P_VB, P_ZB, P_GA, P_GB = range(N_PROJ)


def _sigmoid(x):
    return 0.5 * jnp.tanh(0.5 * x) + 0.5


def _log_sigmoid(x):
    return jnp.minimum(x, 0.0) - jnp.log1p(jnp.exp(-jnp.abs(x)))


def _nt_dot(a, b):
    return lax.dot_general(a, b, (((1,), (1,)), ((), ())), preferred_element_type=F32)


def _dot(a, b):
    return jnp.dot(a, b, preferred_element_type=F32)


def _mod_kernel(c_ref, w_ref, b_ref, o_ref):
    cv = c_ref[...]
    act = cv * _sigmoid(cv)
    o_ref[...] = jnp.dot(act, w_ref[...], precision=HIGHEST, preferred_element_type=F32) + b_ref[...]


def _modulation(cvecs, w_ada, b_ada):
    n_col = 512
    return pl.pallas_call(
        _mod_kernel,
        grid=(3 * D_MODEL // n_col,),
        in_specs=[
            pl.BlockSpec((8, D_MODEL), lambda j: (0, 0)),
            pl.BlockSpec((D_MODEL, n_col), lambda j: (0, j)),
            pl.BlockSpec((1, n_col), lambda j: (0, j)),
        ],
        out_specs=pl.BlockSpec((8, n_col), lambda j: (0, j)),
        out_shape=jax.ShapeDtypeStruct((8, 3 * D_MODEL), F32),
        name="modulation",
    )(cvecs, w_ada, b_ada)


def _segment_scan(x, op, reverse):
    n = x.shape[1]
    lane = lax.broadcasted_iota(jnp.int32, x.shape, 1) & (SCAN - 1)
    fill = 0.0 if op is jnp.add else MASKED
    k = 1
    while k < SCAN:
        if reverse:
            shifted = pltpu.roll(x, n - k, axis=1)
            valid = lane < SCAN - k
        else:
            shifted = pltpu.roll(x, k, axis=1)
            valid = lane >= k
        x = op(x, jnp.where(valid, shifted, fill))
        k *= 2
    return x


def _gate_scalars(g):
    lf = _log_sigmoid(g)
    pre = _segment_scan(lf, jnp.add, reverse=False)
    suf = _segment_scan(lf, jnp.add, reverse=True)
    total = pre + suf - lf
    fg = [slice(d * 8 + H_A, d * 8 + 2 * H_A) for d in range(2)]
    cum = jnp.concatenate([pre[fg[0]], suf[fg[1]]], axis=0)
    bend = jnp.concatenate([total[fg[0]], total[fg[1]]], axis=0)
    rr = jnp.concatenate([g[0:H_A], g[8:8 + H_A]], axis=0) - cum
    fmax = _segment_scan(rr, jnp.maximum, reverse=False)
    bmax = _segment_scan(rr, jnp.maximum, reverse=True)
    mloc = bend + jnp.maximum(fmax, bmax)
    pad = jnp.zeros((H_A, g.shape[1]), F32)
    out = []
    for d in range(2):
        rows = slice(d * H_A, (d + 1) * H_A)
        for q in (cum, rr, fmax if d == 0 else bmax, bend, mloc):
            out += [q[rows], pad]
    return jnp.concatenate(out, axis=0)


def _inproj_kernel(x0_ref, mod0_ref, xn_ref, modn_ref, ng_ref, w_ref, wqvt_ref, wgt_ref, bift_ref, gsgu_ref, ws_ref,
                   bs_ref, wpb_ref, qt_ref, k_ref, vt_ref, oz_ref, sga_ref, pb_ref, gs_ref,
                   yb_scr, hb_scr, *, tm):
    def modulated_norm(x_ref, mod_ref):
        x = x_ref[...]
        xn = x * lax.rsqrt(jnp.mean(x * x, axis=-1, keepdims=True) + EPS) * ng_ref[...]
        return (xn * (1.0 + mod_ref[0, 1:2, :]) + mod_ref[0, 0:1, :]).astype(BF16)

    @pl.when(pl.program_id(0) == 0)
    def _():
        hb_scr[...] = modulated_norm(x0_ref, mod0_ref)

    hb_next = modulated_norm(xn_ref, modn_ref)

    def proj(j):
        return _dot(hb_scr[...], w_ref[:, j * D_MODEL:(j + 1) * D_MODEL])

    za = proj(P_ZA)
    oz_ref[...] = (_sigmoid(proj(P_O)) * (za * _sigmoid(za))).astype(BF16)
    gs_ref[...] = _gate_scalars(_nt_dot(wgt_ref[...], hb_scr[...]) + bift_ref[...])

    vb = proj(P_VB)
    vn = (vb * lax.rsqrt(jnp.mean(vb * vb, axis=-1, keepdims=True) + EPS) * gsgu_ref[...]).astype(BF16)
    u = proj(P_U)
    zb = proj(P_ZB)
    uz = u * (zb * _sigmoid(zb))
    for c in range(tm // CHUNK):
        rows = slice(c * CHUNK, (c + 1) * CHUNK)
        for g in range(G_B):
            cols = slice(g * DG_B, (g + 1) * DG_B)
            s = _dot(ws_ref[g], vn[rows, cols]) + bs_ref[:, cols]
            yb_scr[rows, cols] = (uz[rows, cols] * s).astype(BF16)
    gate_b = _sigmoid(proj(P_GB))
    sga_ref[...] = _sigmoid(proj(P_GA)).astype(BF16)
    qt_ref[...] = (_nt_dot(wqvt_ref[0], hb_scr[...]) * (DH_A ** -0.5)).astype(BF16)
    vt_ref[...] = _nt_dot(wqvt_ref[1], hb_scr[...]).astype(BF16)
    k_ref[...] = proj(P_K).astype(BF16)
    hb_scr[...] = hb_next
    pb_ref[...] = (gate_b * _dot(yb_scr[...], wpb_ref[...])).astype(BF16)


def _inproj(x2d, mod3, seq_tiles, ng, w, wqvt, wgt, bift, gsgu, ws, bs_full, wpb, *, tm):
    n_tok = x2d.shape[0]
    n_tiles = n_tok // tm
    const2 = lambda i: (0, 0)
    const3 = lambda i: (0, 0, 0)
    resident = dict(pipeline_mode=pl.Buffered(1))
    tok_spec = pl.BlockSpec((tm, D_MODEL), lambda i: (i, 0))
    feat_spec = pl.BlockSpec((D_MODEL, tm), lambda i: (0, i))
    next_tile = lambda i: jnp.minimum(i + 1, n_tiles - 1)
    if mod3.shape[0] == 1:
        next_mod_map = const3
    else:
        next_mod_map = lambda i: (next_tile(i) // seq_tiles, 0, 0)
    tok_out = jax.ShapeDtypeStruct((n_tok, D_MODEL), BF16)
    feat_out = jax.ShapeDtypeStruct((D_MODEL, n_tok), BF16)
    return pl.pallas_call(
        functools.partial(_inproj_kernel, tm=tm),
        grid=(n_tiles,),
        in_specs=[
            pl.BlockSpec((tm, D_MODEL), const2),
            pl.BlockSpec((1, 3, D_MODEL), const3),
            pl.BlockSpec((tm, D_MODEL), lambda i: (next_tile(i), 0)),
            pl.BlockSpec((1, 3, D_MODEL), next_mod_map),
            pl.BlockSpec((1, D_MODEL), const2),
            pl.BlockSpec((D_MODEL, N_PROJ * D_MODEL), const2, **resident),
            pl.BlockSpec((2, D_MODEL, D_MODEL), const3, **resident),
            pl.BlockSpec((N_GATES, D_MODEL), const2),
            pl.BlockSpec((N_GATES, 1), const2),
            pl.BlockSpec((1, D_MODEL), const2),
            pl.BlockSpec((G_B, CHUNK, CHUNK), const3),
            pl.BlockSpec((CHUNK, D_MODEL), const2),
            pl.BlockSpec((D_MODEL, D_MODEL), const2, **resident),
        ],
        out_specs=[feat_spec, tok_spec, feat_spec, tok_spec, tok_spec, tok_spec,
                   pl.BlockSpec((GS_ROWS, tm), lambda i: (0, i))],
        out_shape=[feat_out, tok_out, feat_out, tok_out, tok_out, tok_out,
                   jax.ShapeDtypeStruct((GS_ROWS, n_tok), F32)],
        scratch_shapes=[pltpu.VMEM((tm, D_MODEL), BF16),
                        pltpu.VMEM((tm, D_MODEL), BF16)],
        compiler_params=pltpu.CompilerParams(
            dimension_semantics=("arbitrary",), vmem_limit_bytes=VMEM_LIMIT_BYTES),
        name="inproj",
    )(x2d, mod3, x2d, mod3, ng, w, wqvt, wgt, bift, gsgu, ws, bs_full, wpb)


def _store_state_bf16(kcn_scr, d, h, c_val, n_val):
    kcn_scr[d, h, SCAN:SCAN + DH_A, :] = c_val.astype(BF16)
    kcn_scr[d, h, SCAN + DH_A:SCAN + CN_ROWS, :] = jnp.broadcast_to(n_val, (BF16_ROWS, DH_A)).astype(BF16)


def _mlstm_sweep(d, tblk, qt_ref, k_ref, vt_ref, gs_ref, c_scr, n_scr, m_scr, kcn_scr, emit):
    n_seg = tblk // SCAN
    ri = lax.broadcasted_iota(jnp.int32, (SCAN, SCAN), 0)
    ci = lax.broadcasted_iota(jnp.int32, (SCAN, SCAN), 1)
    mask_st = (ri <= ci) if d == 0 else (ri >= ci)

    for i in range(n_seg):
        c = i if d == 0 else n_seg - 1 - i
        seg = slice(c * SCAN, (c + 1) * SCAN)

        def gate_scalar(q):
            g0 = d * GS_PER_DIR + q * 8
            return gs_ref[g0:g0 + H_A, seg]

        cum = gate_scalar(GS_CUM)
        rr = gate_scalar(GS_RR)
        b_end = gate_scalar(GS_BEND)
        m_prev = m_scr[d]
        a = cum + m_prev
        m_t = jnp.maximum(a, cum + gate_scalar(GS_CMAX))
        u = cum - m_t
        sc = jnp.exp(a - m_t)
        em = jnp.exp(-m_t)
        m_new = jnp.maximum(b_end + m_prev, gate_scalar(GS_MLOC))
        e_row = jnp.exp(b_end + rr - m_new)
        sp = jnp.exp(b_end + m_prev - m_new)[:, 0:1]
        m_scr[d] = m_new

        for h in range(H_A):
            hs = slice(h * DH_A, (h + 1) * DH_A)
            row = slice(h, h + 1)
            qt = qt_ref[hs, seg]
            kh = k_ref[seg, hs]
            vt = vt_ref[hs, seg]

            rr_st = jnp.broadcast_to(rr[row], (SCAN, SCAN)).T
            decay = jnp.exp(jnp.where(mask_st, u[row] + rr_st, MASKED))
            kcn_scr[d, h, 0:SCAN, :] = kh
            kcq = _dot(kcn_scr[d, h], qt)
            s_t = kcq[0:SCAN] * decay
            cq = kcq[SCAN:SCAN + DH_A]
            den = jnp.sum(s_t, axis=0, keepdims=True) + sc[row] * kcq[SCAN + DH_A:SCAN + DH_A + 1]
            inv = 1.0 / jnp.maximum(jnp.abs(den), em[row])
            num = _dot(vt, s_t.astype(BF16)) + sc[row] * cq
            emit(c, h, num * inv)

            vte = (vt.astype(F32) * e_row[row]).astype(BF16)
            lhs = jnp.concatenate([vte, jnp.broadcast_to(e_row[row], (BF16_ROWS, SCAN)).astype(BF16)], axis=0)
            upd = _dot(lhs, kh)
            c_new = sp[row] * c_scr[d, h] + upd[0:DH_A]
            n_new = sp[row] * n_scr[d, row, :] + upd[DH_A:DH_A + 1]
            c_scr[d, h] = c_new
            n_scr[d, row, :] = n_new
            _store_state_bf16(kcn_scr, d, h, c_new, n_new)


def _mixer_kernel(*refs, nblk, tblk, has_init, emit_state):
    refs = list(refs)
    qt_ref, k_ref, vt_ref, oz_ref, sga_ref, pb_ref, gs_ref, x_ref, mod_ref = refs[:9]
    refs = refs[9:]
    if has_init:
        c0_ref, n0_ref, m0_ref = refs[:3]
        refs = refs[3:]
    wpa_ref, wout_ref, nf_ref, y_ref = refs[:4]
    refs = refs[4:]
    if emit_state:
        cf_ref, nfin_ref, mfin_ref = refs[:3]
        refs = refs[3:]
    c_scr, n_scr, m_scr, kcn_scr, hb_scr, ya_scr = refs

    n_seg = tblk // SCAN
    sweep = functools.partial(_mlstm_sweep, tblk=tblk, qt_ref=qt_ref, k_ref=k_ref, vt_ref=vt_ref, gs_ref=gs_ref,
                              c_scr=c_scr, n_scr=n_scr, m_scr=m_scr, kcn_scr=kcn_scr)

    def init_state():
        if has_init:
            c_scr[...] = c0_ref[0]
            n_scr[...] = n0_ref[0]
            m_scr[...] = m0_ref[0]
        else:
            c_scr[...] = jnp.zeros_like(c_scr)
            n_scr[...] = jnp.zeros_like(n_scr)
            m_scr[...] = jnp.zeros_like(m_scr)
        for d in range(2):
            for h in range(H_A):
                _store_state_bf16(kcn_scr, d, h, c_scr[d, h], n_scr[d, h:h + 1, :])

    def backward(blk):
        def emit(c, h, h_t):
            hb_scr[blk * n_seg + c, h * DH_A:(h + 1) * DH_A, :] = h_t.astype(BF16)

        sweep(1, emit=emit)

    def forward(blk):
        def emit(c, h, h_t):
            hs = slice(h * DH_A, (h + 1) * DH_A)
            rows = slice(c * SCAN, (c + 1) * SCAN)
            h_a = (h_t + hb_scr[blk * n_seg + c, hs, :].astype(F32)).T
            ya_scr[rows, hs] = (oz_ref[rows, hs].astype(F32) * h_a).astype(BF16)

        sweep(0, emit=emit)

        pa = _dot(ya_scr[...], wpa_ref[...])
        merged = sga_ref[...].astype(F32) * pa + pb_ref[...].astype(F32)
        upd = _dot(merged.astype(BF16), wout_ref[...])
        out = x_ref[...] + mod_ref[0, 2:3, :] * upd
        y_ref[...] = out * lax.rsqrt(jnp.mean(out * out, axis=-1, keepdims=True) + EPS) * nf_ref[...]

    def write_state():
        cf_ref[0] = c_scr[...]
        nfin_ref[0] = n_scr[...]
        mfin_ref[0] = m_scr[...]

    if nblk == 1:
        init_state()
        backward(0)
        forward(0)
        if emit_state:
            write_state()
        return

    step = pl.program_id(1)
    pl.when(step == 0)(init_state)
    pl.when(step < nblk)(lambda: backward(nblk - 1 - step))
    pl.when(step >= nblk)(lambda: forward(step - nblk))
    if emit_state:
        pl.when(step == 2 * nblk - 1)(write_state)


def _mixer(qt, k, vt, oz, sga, pb, gs, x2d, mod3, init_state, wpa, wout, norm_f, *, n_seq, seq_len, tblk,
           emit_state):
    nblk = seq_len // tblk
    has_init = init_state is not None
    n_steps = 1 if nblk == 1 else 2 * nblk

    def sweep_blk(b, s):
        if nblk == 1:
            return b
        return b * nblk + jnp.where(s < nblk, nblk - 1 - s, s - nblk)

    def tail_blk(b, s):
        if nblk == 1:
            return b
        return b * nblk + jnp.maximum(s - nblk, 0)

    sweep_tok = pl.BlockSpec((tblk, D_MODEL), lambda b, s: (sweep_blk(b, s), 0))
    sweep_feat = pl.BlockSpec((D_MODEL, tblk), lambda b, s: (0, sweep_blk(b, s)))
    tail_spec = pl.BlockSpec((tblk, D_MODEL), lambda b, s: (tail_blk(b, s), 0))
    const2 = lambda b, s: (0, 0)
    if mod3.shape[0] == 1:
        mod_map = lambda b, s: (0, 0, 0)
    else:
        mod_map = lambda b, s: (b, 0, 0)
    c_spec = pl.BlockSpec((1, 2, H_A, DH_A, DH_A), lambda b, s: (b, 0, 0, 0, 0))
    n_spec = pl.BlockSpec((1, 2, H_A, DH_A), lambda b, s: (b, 0, 0, 0))
    m_spec = pl.BlockSpec((1, 2, H_A, SCAN), lambda b, s: (b, 0, 0, 0))

    in_specs = [
        sweep_feat, sweep_tok, sweep_feat,
        tail_spec, tail_spec, tail_spec,
        pl.BlockSpec((GS_ROWS, tblk), lambda b, s: (0, sweep_blk(b, s))),
        tail_spec,
        pl.BlockSpec((1, 3, D_MODEL), mod_map),
    ]
    args = [qt, k, vt, oz, sga, pb, gs, x2d, mod3]
    if has_init:
        in_specs += [c_spec, n_spec, m_spec]
        args += list(init_state)
    in_specs += [
        pl.BlockSpec((D_MODEL, D_MODEL), const2),
        pl.BlockSpec((D_MODEL, D_MODEL), const2),
        pl.BlockSpec((1, D_MODEL), const2),
    ]
    args += [wpa, wout, norm_f]

    out_specs = [tail_spec]
    out_shape = [jax.ShapeDtypeStruct((n_seq * seq_len, D_MODEL), F32)]
    if emit_state:
        out_specs += [c_spec, n_spec, m_spec]
        out_shape += [
            jax.ShapeDtypeStruct((n_seq, 2, H_A, DH_A, DH_A), F32),
            jax.ShapeDtypeStruct((n_seq, 2, H_A, DH_A), F32),
            jax.ShapeDtypeStruct((n_seq, 2, H_A, SCAN), F32),
        ]

    return pl.pallas_call(
        functools.partial(_mixer_kernel, nblk=nblk, tblk=tblk, has_init=has_init, emit_state=emit_state),
        grid=(n_seq, n_steps),
        in_specs=in_specs,
        out_specs=out_specs,
        out_shape=out_shape,
        scratch_shapes=[
            pltpu.VMEM((2, H_A, DH_A, DH_A), F32),
            pltpu.VMEM((2, H_A, DH_A), F32),
            pltpu.VMEM((2, H_A, SCAN), F32),
            pltpu.VMEM((2, H_A, SCAN + CN_ROWS, DH_A), BF16),
            pltpu.VMEM((seq_len // SCAN, D_MODEL, SCAN), BF16),
            pltpu.VMEM((tblk, D_MODEL), BF16),
        ],
        compiler_params=pltpu.CompilerParams(
            dimension_semantics=("arbitrary", "arbitrary"), vmem_limit_bytes=VMEM_LIMIT_BYTES),
        name="mixer",
    )(*args)


def _prep_w_in_kernel(wt_ref, w_ref, wqvt_ref):
    j = pl.program_id(0)
    wt = wt_ref[...]
    w_ref[...] = wt.T.astype(BF16)

    @pl.when(jnp.logical_or(j == P_Q, j == P_V))
    def _():
        wqvt_ref[0] = wt.astype(BF16)


def _prep_w_in(w_t):
    gate0 = N_PROJ // 2 * D_MODEL

    def row_start(j):
        return (j * (D_MODEL // 8) + (j // (N_PROJ // 2)) * (N_GATES // 8)) * 8

    return pl.pallas_call(
        _prep_w_in_kernel,
        grid=(N_PROJ,),
        in_specs=[pl.BlockSpec((pl.Element(D_MODEL), pl.Element(D_MODEL)), lambda j: (row_start(j), 0))],
        out_specs=[pl.BlockSpec((D_MODEL, D_MODEL), lambda j: (0, j)),
                   pl.BlockSpec((1, D_MODEL, D_MODEL), lambda j: (jnp.minimum(j // P_V, 1), 0, 0))],
        out_shape=[jax.ShapeDtypeStruct((D_MODEL, N_PROJ * D_MODEL), BF16),
                   jax.ShapeDtypeStruct((2, D_MODEL, D_MODEL), BF16)],
        compiler_params=pltpu.CompilerParams(dimension_semantics=("arbitrary",)),
        name="prep_w_in",
    )(w_t)


def kernel(x_prompt, x_sample, c, state_C, state_n, state_m, c_ctx, norm_g, w_ada, b_ada, w_in, b_if, g_sgu, w_s, b_s,
           w_proj_a, w_proj_b, w_out, norm_f):
    n_ctx, t_ctx, _ = x_prompt.shape
    n_lat, t_lat, _ = x_sample.shape
    depth = norm_g.shape[0]
    assert depth == 1, "single-layer step"
    l = 0

    cvecs = jnp.concatenate([c, c_ctx[None], jnp.zeros((8 - n_lat - 1, D_MODEL), F32)], axis=0)
    mod = _modulation(cvecs, w_ada[l], b_ada[l][None]).reshape(8, 3, D_MODEL)
    mod_lat = mod[:n_lat]
    mod_ctx = mod[n_lat:n_lat + 1]

    w_t = jnp.transpose(w_in[l])
    w, wqvt = _prep_w_in(w_t)
    gate0 = N_PROJ // 2 * D_MODEL
    wgt = w_t[gate0:gate0 + N_GATES].astype(BF16)
    bift = b_if[l].reshape(N_GATES, 1)
    ng = norm_g[l][None]
    gsgu = g_sgu[l][None]
    ws = w_s[l].astype(BF16)
    bs_full = jnp.repeat(jnp.transpose(b_s[l]), DG_B, axis=1)
    wpa = w_proj_a[l].astype(BF16)
    wpb = w_proj_b[l].astype(BF16)
    wout = w_out[l].astype(BF16)
    nf = norm_f[None]

    tm = 512
    xp2 = x_prompt.reshape(n_ctx * t_ctx, D_MODEL)
    xs2 = x_sample.reshape(n_lat * t_lat, D_MODEL)
    shared = (ng, w, wqvt, wgt, bift, gsgu, ws, bs_full, wpb)
    pc = _inproj(xp2, mod_ctx, 1, *shared, tm=tm)
    ps = _inproj(xs2, mod_lat, t_lat // tm, *shared, tm=tm)

    y_p, c_new, n_new, m_new = _mixer(*pc, xp2, mod_ctx, None, wpa, wout, nf, n_seq=n_ctx, seq_len=t_ctx,
                                      tblk=t_ctx, emit_state=True)
    init = (state_C[:, l], state_n[:, l],
            jnp.broadcast_to(state_m[:, l][..., None], (n_lat, 2, H_A, SCAN)))
    (y_s,) = _mixer(*ps, xs2, mod_lat, init, wpa, wout, nf, n_seq=n_lat, seq_len=t_lat, tblk=512,
                    emit_state=False)

    return (y_p.reshape(n_ctx, t_ctx, D_MODEL), y_s.reshape(n_lat, t_lat, D_MODEL),
            c_new[:, None], n_new[:, None], m_new[:, None, :, :, 0])
```
